```python
import math
import jax, jax.numpy as jnp
from jax import lax
import numpy as np

D_MODEL = 1024
BATCH = 16
SEQ = 2048
DEPTH = 2
DEC_BATCH = 16
DEC_SEQ = 4096
PAST_LEN = 128

GRID_W = 64
HEAD_DIM = 64
Q_BLOCK = 128
EPS = 1e-6
A_Q_HEADS = 8
A_KV_HEADS = 2
AXIAL_THETA = 10000.0
B_HEADS = 8
NA_ROWS = 8
NA_COLS = 16
C_HEADS = D_MODEL // (2 * HEAD_DIM)
ROPE_THETA = 500000.0
ROPE_DIMS = HEAD_DIM // 4
A_Q_W = A_Q_HEADS * HEAD_DIM
A_KV_W = A_KV_HEADS * HEAD_DIM
B_W = B_HEADS * HEAD_DIM
EVEN_IN = A_Q_W + 2 * A_KV_W + 3 * B_W
EVEN_OUT = A_Q_W + B_W
C_QK_W = C_HEADS * 2 * HEAD_DIM
ODD_IN = 3 * C_QK_W
ODD_OUT = C_HEADS * 2 * HEAD_DIM
N_GROUPS = 4
EXPERTS_PER_GROUP = 8
N_EXPERTS = N_GROUPS * EXPERTS_PER_GROUP
TOP_K = 2
D_EXPERT = 512
N_EVEN = (DEPTH + 1) // 2
N_ODD = DEPTH // 2

kernel_name = 'hybrid_axial_gqa_natten_diffattn_hmoe_encoder'


def rms_norm(x, g):
    xf = x.astype(jnp.float32)
    y = xf * lax.rsqrt(jnp.mean(xf * xf, axis=-1, keepdims=True) + EPS)
    return (y * g.astype(jnp.float32)).astype(x.dtype)


def rope(x, pos, theta):
    n = x.shape[-1]
    inv = theta ** (-jnp.arange(0, n, 2, dtype=jnp.float32) / n)
    ang = pos[:, None] * inv[None, :]
    cos = jnp.concatenate([jnp.cos(ang)] * 2, -1).astype(x.dtype)
    sin = jnp.concatenate([jnp.sin(ang)] * 2, -1).astype(x.dtype)
    x1, x2 = x[..., : n // 2], x[..., n // 2:]
    return x * cos + jnp.concatenate([-x2, x1], -1) * sin


def axial_rope(x, row, col):
    half = x.shape[-1] // 2
    return jnp.concatenate([rope(x[..., :half], row, AXIAL_THETA), rope(x[..., half:], col, AXIAL_THETA)], -1)


def partial_rope(x, pos):
    return jnp.concatenate([rope(x[..., :ROPE_DIMS], pos, ROPE_THETA), x[..., ROPE_DIMS:]], -1)


def gqa_axial_attention(q, k, v, q_gain, k_gain, row, col):
    bn, L = q.shape[0], q.shape[1]
    d = HEAD_DIM
    grp = A_Q_HEADS // A_KV_HEADS
    nb = L // Q_BLOCK
    q = axial_rope(rms_norm(q, q_gain).transpose(0, 2, 1, 3), row, col)
    k = axial_rope(rms_norm(k, k_gain).transpose(0, 2, 1, 3), row, col)
    v = v.transpose(0, 2, 1, 3)
    qb = q.reshape(bn, A_KV_HEADS, grp, nb, Q_BLOCK, d).transpose(3, 0, 1, 2, 4, 5)
    scale = d ** -0.5

    def attend(qblk):
        s = jnp.einsum('bkgqd,bksd->bkgqs', qblk, k).astype(jnp.float32) * scale
        p = jax.nn.softmax(s, axis=-1).astype(v.dtype)
        return jnp.einsum('bkgqs,bksd->bkgqd', p, v)

    o = lax.map(attend, qb)
    return o.transpose(1, 0, 4, 2, 3, 5).reshape(bn, L, A_Q_HEADS * d)


def neighbourhood_attention(q, k, v, rpb):
    bn, L, H, d = q.shape
    rows = L // GRID_W
    kr = min(NA_ROWS, rows)
    cs = np.clip(np.arange(GRID_W) - NA_COLS // 2, 0, GRID_W - NA_COLS)
    col_idx = cs[:, None] + np.arange(NA_COLS)[None, :]
    col_off = col_idx - np.arange(GRID_W)[:, None] + (NA_COLS - 1)
    qg = q.reshape(bn, rows, GRID_W, H, d).transpose(1, 0, 2, 3, 4)
    kg = k.reshape(bn, rows, GRID_W, H, d)
    vg = v.reshape(bn, rows, GRID_W, H, d)
    scale = d ** -0.5

    def na_row(args):
        r, q_row = args
        rs = jnp.clip(r - kr // 2, 0, rows - kr)
        kw = lax.dynamic_slice_in_dim(kg, rs, kr, axis=1)[:, :, col_idx]
        vw = lax.dynamic_slice_in_dim(vg, rs, kr, axis=1)[:, :, col_idx]
        row_off = rs + jnp.arange(kr) - r + (NA_ROWS - 1)
        bias = rpb[:, row_off[None, :, None], col_off[:, None, :]]
        s = jnp.einsum('bchd,brcjhd->bhcrj', q_row, kw).astype(jnp.float32) * scale + bias[None].astype(jnp.float32)
        p = jax.nn.softmax(s.reshape(bn, H, GRID_W, kr * NA_COLS), axis=-1).reshape(s.shape).astype(v.dtype)
        return jnp.einsum('bhcrj,brcjhd->bchd', p, vw)

    o = lax.map(na_row, (jnp.arange(rows), qg))
    return o.transpose(1, 0, 2, 3, 4).reshape(bn, L, H * d)


def diff_attention(q, k, v, lq1, lk1, lq2, lk2, sub_gain, pos, lam_init):
    bn, L = q.shape[0], q.shape[1]
    d = HEAD_DIM
    nb = L // Q_BLOCK
    q = partial_rope(q.transpose(0, 2, 3, 1, 4), pos)
    k = partial_rope(k.transpose(0, 2, 3, 1, 4), pos)
    v = v.transpose(0, 2, 1, 3)
    f32 = jnp.float32
    lam = (jnp.exp(jnp.sum(lq1.astype(f32) * lk1.astype(f32))) - jnp.exp(jnp.sum(lq2.astype(f32) * lk2.astype(f32))) + lam_init)
    qb = q.reshape(bn, C_HEADS, 2, nb, Q_BLOCK, d).transpose(3, 0, 1, 2, 4, 5)
    scale = d ** -0.5

    def attend(qblk):
        s = jnp.einsum('bhmqd,bhmsd->bhmqs', qblk, k).astype(f32) * scale
        p = jax.nn.softmax(s, axis=-1)
        a = (p[:, :, 0] - lam * p[:, :, 1]).astype(v.dtype)
        return jnp.einsum('bhqs,bhse->bhqe', a, v)

    o = lax.map(attend, qb)
    o = (rms_norm(o, sub_gain) * (1.0 - lam_init)).astype(v.dtype)
    return o.transpose(1, 0, 3, 2, 4).reshape(bn, L, C_HEADS * 2 * d)


def hier_moe(x, w_rg, b_rg, w_re, b_re, w_gate, w_up, w_down):
    shp = x.shape
    t = x.reshape(-1, shp[-1])
    f32 = jnp.float32
    gl = (t @ w_rg).astype(f32) + b_rg.astype(f32)
    gp = jax.nn.softmax(gl, axis=-1)
    g_idx = jnp.argmax(gl, axis=-1)
    g_w = jnp.take_along_axis(gp, g_idx[:, None], axis=-1)
    el = ((t @ w_re).astype(f32) + b_re.astype(f32)).reshape(-1, N_GROUPS, EXPERTS_PER_GROUP)
    el = jnp.take_along_axis(el, g_idx[:, None, None], axis=1)[:, 0]
    top_w, top_i = lax.top_k(jax.nn.softmax(el, axis=-1), TOP_K)
    top_w = top_w / jnp.sum(top_w, axis=-1, keepdims=True) * g_w
    e_idx = g_idx[:, None] * EXPERTS_PER_GROUP + top_i
    gates = jnp.sum(jax.nn.one_hot(e_idx, N_EXPERTS, dtype=f32) * top_w[..., None], axis=1)
    out = jnp.zeros(t.shape, f32)
    for e in range(N_EXPERTS):
        h = jax.nn.silu(t @ w_gate[e]) * (t @ w_up[e])
        out = out + gates[:, e:e + 1] * (h @ w_down[e]).astype(f32)
    return out.astype(x.dtype).reshape(shp)


def trunk(x, norm_mix, norm_ffn, norm_final, w_in_even, q_gain, k_gain, rpb, w_out_even,
          w_in_odd, lam_q1, lam_k1, lam_q2, lam_k2, subln_gain, w_out_odd,
          w_rg, b_rg, w_re, b_re, w_gate, w_up, w_down):
    bn, L, _ = x.shape
    t = jnp.arange(L)
    row = (t // GRID_W).astype(jnp.float32)
    col = (t % GRID_W).astype(jnp.float32)
    pos = t.astype(jnp.float32)
    for i in range(DEPTH):
        j = i // 2
        h = rms_norm(x, norm_mix[i])
        if i % 2 == 0:
            proj = h @ w_in_even[j]
            cuts = np.cumsum([A_Q_W, A_KV_W, A_KV_W, B_W, B_W])
            aq, ak, av, bq, bk, bv = jnp.split(proj, cuts, axis=-1)
            oa = gqa_axial_attention(aq.reshape(bn, L, A_Q_HEADS, HEAD_DIM), ak.reshape(bn, L, A_KV_HEADS, HEAD_DIM), av.reshape(bn, L, A_KV_HEADS, HEAD_DIM), q_gain[j], k_gain[j], row, col)
            ob = neighbourhood_attention(bq.reshape(bn, L, B_HEADS, HEAD_DIM), bk.reshape(bn, L, B_HEADS, HEAD_DIM), bv.reshape(bn, L, B_HEADS, HEAD_DIM), rpb[j])
            o = jnp.concatenate([oa, ob], axis=-1) @ w_out_even[j]
        else:
            proj = h @ w_in_odd[j]
            cq, ck, cv = jnp.split(proj, 3, axis=-1)
            lam_init = 0.8 - 0.6 * math.exp(-0.3 * i)
            oc = diff_attention(cq.reshape(bn, L, C_HEADS, 2, HEAD_DIM), ck.reshape(bn, L, C_HEADS, 2, HEAD_DIM), cv.reshape(bn, L, C_HEADS, 2 * HEAD_DIM), lam_q1[j], lam_k1[j], lam_q2[j], lam_k2[j], subln_gain[j], pos, lam_init)
            o = oc @ w_out_odd[j]
        x = x + o
        x = x + hier_moe(rms_norm(x, norm_ffn[i]), w_rg[i], b_rg[i], w_re[i], b_re[i], w_gate[i], w_up[i], w_down[i])
    return rms_norm(x, norm_final)


def setup_inputs(seed: int = 0) -> dict:
    key = jax.random.key(seed)
    ks = jax.random.split(key, 24)
    f32 = jnp.float32

    def nrm(k, shape, scale):
        return jax.random.normal(k, shape, f32) * scale

    def gain(k, shape):
        return 1.0 + 0.02 * jax.random.normal(k, shape, f32)

    D = D_MODEL
    return {
        'x_prompt': jax.random.normal(ks[0], (BATCH, SEQ, D), f32),
        'x_sample': jax.random.normal(ks[1], (DEC_BATCH, DEC_SEQ, D), f32),
        'norm_mix': gain(ks[2], (DEPTH, D)),
        'norm_ffn': gain(ks[3], (DEPTH, D)),
        'norm_final': gain(ks[4], (D,)),
        'w_in_even': nrm(ks[5], (N_EVEN, D, EVEN_IN), D ** -0.5),
        'q_gain': gain(ks[6], (N_EVEN, HEAD_DIM)),
        'k_gain': gain(ks[7], (N_EVEN, HEAD_DIM)),
        'rpb': nrm(ks[8], (N_EVEN, B_HEADS, 2 * NA_ROWS - 1, 2 * NA_COLS - 1), 0.1),
        'w_out_even': nrm(ks[9], (N_EVEN, EVEN_OUT, D), EVEN_OUT ** -0.5),
        'w_in_odd': nrm(ks[10], (N_ODD, D, ODD_IN), D ** -0.5),
        'lam_q1': nrm(ks[11], (N_ODD, HEAD_DIM), 0.1),
        'lam_k1': nrm(ks[12], (N_ODD, HEAD_DIM), 0.1),
        'lam_q2': nrm(ks[13], (N_ODD, HEAD_DIM), 0.1),
        'lam_k2': nrm(ks[14], (N_ODD, HEAD_DIM), 0.1),
        'subln_gain': gain(ks[15], (N_ODD, 2 * HEAD_DIM)),
        'w_out_odd': nrm(ks[16], (N_ODD, ODD_OUT, D), ODD_OUT ** -0.5),
        'w_rg': nrm(ks[17], (DEPTH, D, N_GROUPS), D ** -0.5),
        'b_rg': nrm(ks[18], (DEPTH, N_GROUPS), 0.01),
        'w_re': nrm(ks[19], (DEPTH, D, N_EXPERTS), D ** -0.5),
        'b_re': nrm(ks[20], (DEPTH, N_EXPERTS), 0.01),
        'w_gate': nrm(ks[21], (DEPTH, N_EXPERTS, D, D_EXPERT), D ** -0.5),
        'w_up': nrm(ks[22], (DEPTH, N_EXPERTS, D, D_EXPERT), D ** -0.5),
        'w_down': nrm(ks[23], (DEPTH, N_EXPERTS, D_EXPERT, D), D_EXPERT ** -0.5),
    }


def reference(x_prompt, x_sample, norm_mix, norm_ffn, norm_final, w_in_even, q_gain, k_gain, rpb, w_out_even,
              w_in_odd, lam_q1, lam_k1, lam_q2, lam_k2, subln_gain, w_out_odd,
              w_rg, b_rg, w_re, b_re, w_gate, w_up, w_down):
    y_prompt = trunk(x_prompt, norm_mix, norm_ffn, norm_final, w_in_even, q_gain, k_gain, rpb, w_out_even,
                     w_in_odd, lam_q1, lam_k1, lam_q2, lam_k2, subln_gain, w_out_odd,
                     w_rg, b_rg, w_re, b_re, w_gate, w_up, w_down)
    y_sample = trunk(x_sample, norm_mix, norm_ffn, norm_final, w_in_even, q_gain, k_gain, rpb, w_out_even,
                     w_in_odd, lam_q1, lam_k1, lam_q2, lam_k2, subln_gain, w_out_odd,
                     w_rg, b_rg, w_re, b_re, w_gate, w_up, w_down)
    return (y_prompt, y_sample)
```

```python
import functools
import math

import numpy as np
import jax
import jax.numpy as jnp
from jax import lax
from jax.experimental import pallas as pl
from jax.experimental.pallas import tpu as pltpu

F32 = jnp.float32
BF16 = jnp.bfloat16
I32 = jnp.int32

D_MODEL = 1024
HEAD_DIM = 64
GRID_W = 64
EPS = 1e-6
LANES = 128
A_Q_W, A_KV_W, B_W = 512, 128, 512
EVEN_IN = A_Q_W + 2 * A_KV_W + 3 * B_W
ODD_IN = 3 * D_MODEL
AXIAL_THETA = 10000.0
ROPE_THETA = 500000.0
ROPE_DIMS = HEAD_DIM // 4
NA_ROWS, NA_COLS = 8, 16
NA_QROWS = 4
NA_KROWS = NA_QROWS + NA_ROWS
N_GROUPS, EXPERTS_PER_GROUP = 4, 8
N_EXPERTS = N_GROUPS * EXPERTS_PER_GROUP
D_EXPERT = 512
SCALE = HEAD_DIM ** -0.5
NEG = -1e30

VMEM_LIMIT_BYTES = 56 * 1024 * 1024
TOK_TILE = 512
ROUTE_TILE = 256
EXPERT_TILE = 512
ATT_Q = 256


def _params(*sem):
    return pltpu.CompilerParams(dimension_semantics=sem, vmem_limit_bytes=VMEM_LIMIT_BYTES)


def _lane_iota(shape=(1, LANES)):
    return lax.broadcasted_iota(I32, shape, len(shape) - 1)


def _rms(x, gain):
    return x * lax.rsqrt(jnp.mean(x * x, axis=-1, keepdims=True) + EPS) * gain


def _dot(a, b):
    return jnp.dot(a, b, preferred_element_type=F32)


def _dot_nt(a, b):
    return lax.dot_general(a, b, (((1,), (1,)), ((), ())), preferred_element_type=F32)


def _rope(y, tab_ref, shift):
    return (y * tab_ref[0] + pltpu.roll(y, LANES - shift, 1) * tab_ref[1]
            + pltpu.roll(y, shift, 1) * tab_ref[2])


def _rope_tables(angles, n):
    cos = jnp.cos(angles)
    sin = jnp.sin(angles)
    low = jnp.asarray((np.arange(HEAD_DIM) % n) < n // 2)
    tab = jnp.stack([cos, jnp.where(low, -sin, 0.0), jnp.where(low, 0.0, sin)])
    return jnp.concatenate([tab, tab], axis=-1).astype(F32)


def _axial_tables(max_len):
    pos = jnp.arange(max_len)
    row = (pos // GRID_W).astype(F32)
    col = (pos % GRID_W).astype(F32)
    half = HEAD_DIM // 2
    inv = AXIAL_THETA ** (-jnp.arange(0, half, 2, dtype=F32) / half)
    inv2 = jnp.concatenate([inv, inv])
    ang = jnp.concatenate([row[:, None] * inv2[None], col[:, None] * inv2[None]], axis=-1)
    return _rope_tables(ang, half)


def _partial_tables(max_len):
    pos = jnp.arange(max_len).astype(F32)
    inv = ROPE_THETA ** (-jnp.arange(0, ROPE_DIMS, 2, dtype=F32) / ROPE_DIMS)
    inv2 = jnp.concatenate([inv, inv])
    ang = pos[:, None] * inv2[None]
    tab = _rope_tables(jnp.concatenate([ang, jnp.zeros((max_len, HEAD_DIM - ROPE_DIMS), F32)], -1), ROPE_DIMS)
    keep = jnp.asarray(np.tile(np.arange(HEAD_DIM) < ROPE_DIMS, 2))
    return jnp.stack([jnp.where(keep, tab[0], 1.0), jnp.where(keep, tab[1], 0.0), jnp.where(keep, tab[2], 0.0)])


def _natten_bias(rpb):
    ql = np.arange(NA_QROWS * GRID_W)
    kl = np.arange(NA_KROWS * GRID_W)
    rl, c = ql // GRID_W, ql % GRID_W
    ki, kc = kl // GRID_W, kl % GRID_W
    cs = np.clip(c - NA_COLS // 2, 0, GRID_W - NA_COLS)
    col_ok = (kc[None, :] >= cs[:, None]) & (kc[None, :] < cs[:, None] + NA_COLS)
    col_idx = np.clip(kc[None, :] - c[:, None] + NA_COLS - 1, 0, 2 * NA_COLS - 2)
    out = []
    for delta, first in ((0, np.zeros_like(rl)), (NA_QROWS, rl), (2 * NA_QROWS, np.full_like(rl, NA_QROWS))):
        row_ok = (ki[None, :] >= first[:, None]) & (ki[None, :] < first[:, None] + NA_ROWS)
        row_idx = np.clip(ki[None, :] - delta - rl[:, None] + NA_ROWS - 1, 0, 2 * NA_ROWS - 2)
        b = rpb[:, row_idx, col_idx].astype(F32)
        out.append(jnp.where(jnp.asarray(row_ok & col_ok)[None], b, NEG))
    return jnp.stack(out)


def _inproj_even_body(x_ref, g_ref, w_ref, tab_ref, hg_ref, oa_ref, ob_ref):
    hn = _rms(x_ref[...], g_ref[...]).astype(BF16)
    low = _lane_iota() < HEAD_DIM

    def head_norm_rope(y, gain, scale):
        ss = y * y
        s_lo = jnp.sum(jnp.where(low, ss, 0.0), axis=-1, keepdims=True)
        s_hi = jnp.sum(jnp.where(low, 0.0, ss), axis=-1, keepdims=True)
        y = y * lax.rsqrt(jnp.where(low, s_lo, s_hi) * (1.0 / HEAD_DIM) + EPS) * gain
        return _rope(y, tab_ref, HEAD_DIM // 4) * scale

    for j in range(A_Q_W // LANES):
        c = j * LANES
        oa_ref[:, c:c + LANES] = head_norm_rope(_dot(hn, w_ref[:, c:c + LANES]), hg_ref[0:1, :], SCALE).astype(BF16)
    c = A_Q_W
    oa_ref[:, c:c + LANES] = head_norm_rope(_dot(hn, w_ref[:, c:c + LANES]), hg_ref[1:2, :], 1.0).astype(BF16)
    c = A_Q_W + A_KV_W
    oa_ref[:, c:c + LANES] = _dot(hn, w_ref[:, c:c + LANES]).astype(BF16)
    c0 = A_Q_W + 2 * A_KV_W
    for j in range(3 * B_W // 256):
        c = j * 256
        y = _dot(hn, w_ref[:, c0 + c:c0 + c + 256])
        ob_ref[:, c:c + 256] = (y * SCALE if c < B_W else y).astype(BF16)


def _inproj_odd_body(x_ref, g_ref, w_ref, tab_ref, oc_ref):
    hn = _rms(x_ref[...], g_ref[...]).astype(BF16)
    for j in range(2 * D_MODEL // LANES):
        c = j * LANES
        y = _rope(_dot(hn, w_ref[:, c:c + LANES]), tab_ref, ROPE_DIMS // 2)
        oc_ref[:, c:c + LANES] = (y * SCALE if c < D_MODEL else y).astype(BF16)
    for j in range(D_MODEL // 256):
        c = 2 * D_MODEL + j * 256
        oc_ref[:, c:c + 256] = _dot(hn, w_ref[:, c:c + 256]).astype(BF16)


def _tab_index(seqs, tm):
    (r0, _, l0), (r1, _, l1) = seqs

    def index(i):
        return (0, jnp.where(i < r1 // tm, (i - r0 // tm) % (l0 // tm), (i - r1 // tm) % (l1 // tm)), 0)
    return index


def _inproj(x, gain, w, tab, seqs, head_gain=None):
    t = x.shape[0]
    tm = TOK_TILE
    even = head_gain is not None
    in_specs = [pl.BlockSpec((tm, D_MODEL), lambda i: (i, 0)),
                pl.BlockSpec((1, D_MODEL), lambda i: (0, 0)),
                pl.BlockSpec(w.shape, lambda i: (0, 0)),
                pl.BlockSpec((3, tm, LANES), _tab_index(seqs, tm))]
    args = [x, gain.reshape(1, D_MODEL), w, tab]
    if even:
        in_specs.append(pl.BlockSpec((2, LANES), lambda i: (0, 0)))
        args.append(head_gain)
        widths = (A_Q_W + 2 * A_KV_W, 3 * B_W)
        body = _inproj_even_body
    else:
        widths = (ODD_IN,)
        body = _inproj_odd_body
    out = pl.pallas_call(
        body,
        grid=(t // tm,),
        in_specs=in_specs,
        out_specs=[pl.BlockSpec((tm, n), lambda i: (i, 0)) for n in widths],
        out_shape=[jax.ShapeDtypeStruct((t, n), BF16) for n in widths],
        compiler_params=_params("parallel"),
        name="inproj_even" if even else "inproj_odd",
    )(*args)
    return out


def _outproj_body(*refs, n_pieces, first_tiles):
    o0, o1 = refs[:n_pieces], refs[n_pieces:2 * n_pieces]
    w_ref, x_ref, g_ref, wr_ref, br_ref, x1_ref, hn_ref, lg_ref = refs[2 * n_pieces:]
    in_first = pl.program_id(0) < first_tiles
    x1 = x_ref[...]
    c = 0
    for a_ref, b_ref in zip(o0, o1):
        n = a_ref.shape[1]
        x1 = x1 + _dot(jnp.where(in_first, a_ref[...], b_ref[...]), w_ref[c:c + n, :])
        c += n
    x1_ref[...] = x1
    hn = _rms(x1, g_ref[...])
    hn_ref[...] = hn
    lg_ref[...] = jnp.dot(hn, wr_ref[...], precision=lax.Precision.HIGHEST,
                          preferred_element_type=F32) + br_ref[...]


def _outproj(pieces, w, x, gain, w_router, b_router):
    t = x.shape[0]
    tm = TOK_TILE
    row = lambda i: (i, 0)
    fixed = lambda i: (0, 0)
    first_tiles = pieces[0][0].shape[0] // tm
    piece_specs = (
        [pl.BlockSpec((tm, p.shape[1]), lambda i: (jnp.minimum(i, first_tiles - 1), 0)) for p in pieces[0]]
        + [pl.BlockSpec((tm, p.shape[1]), lambda i: (jnp.maximum(i - first_tiles, 0), 0)) for p in pieces[1]])
    return pl.pallas_call(
        functools.partial(_outproj_body, n_pieces=len(pieces[0]), first_tiles=first_tiles),
        grid=(t // tm,),
        in_specs=piece_specs + [
            pl.BlockSpec((D_MODEL, D_MODEL), fixed),
            pl.BlockSpec((tm, D_MODEL), row), pl.BlockSpec((1, D_MODEL), fixed),
            pl.BlockSpec((D_MODEL, LANES), fixed), pl.BlockSpec((1, LANES), fixed)],
        out_specs=[pl.BlockSpec((tm, D_MODEL), row), pl.BlockSpec((tm, D_MODEL), row),
                   pl.BlockSpec((tm, LANES), row)],
        out_shape=[jax.ShapeDtypeStruct((t, D_MODEL), F32), jax.ShapeDtypeStruct((t, D_MODEL), F32),
                   jax.ShapeDtypeStruct((t, LANES), F32)],
        compiler_params=_params("parallel"),
        name="outproj_router",
    )(*pieces[0], *pieces[1], w, x, gain.reshape(1, D_MODEL), w_router, b_router)


def _final_norm_body(x_ref, g_ref, o_ref):
    o_ref[...] = _rms(x_ref[...], g_ref[...])


def _final_norm(x, gain, r0, rows):
    tm = TOK_TILE
    return pl.pallas_call(
        _final_norm_body,
        grid=(rows // tm,),
        in_specs=[pl.BlockSpec((tm, D_MODEL), lambda i: (i + r0 // tm, 0)),
                  pl.BlockSpec((1, D_MODEL), lambda i: (0, 0))],
        out_specs=pl.BlockSpec((tm, D_MODEL), lambda i: (i, 0)),
        out_shape=jax.ShapeDtypeStruct((rows, D_MODEL), F32),
        compiler_params=_params("parallel"),
        name="final_norm",
    )(x, gain.reshape(1, D_MODEL))


def _softmax_pv(s, v):
    m = jnp.max(s, axis=-1, keepdims=True)
    p = jnp.exp(s - m)
    l = jnp.sum(p, axis=-1, keepdims=True)
    return _dot(p.astype(BF16), v) / l


def _gqa_body(q_ref, k_ref, v_ref, o_ref):
    k = k_ref[...]
    v = v_ref[...]
    lane = _lane_iota()
    for j in range(A_Q_W // LANES):
        g = j // 2
        keep = (lane >= g * HEAD_DIM) & (lane < (g + 1) * HEAD_DIM)
        q2 = q_ref[:, j * LANES:(j + 1) * LANES].astype(F32)
        res = []
        for par in range(2):
            qh = q2 if par == g else pltpu.roll(q2, HEAD_DIM, 1)
            qh = jnp.where(keep, qh, 0.0).astype(BF16)
            o = _softmax_pv(_dot_nt(qh, k), v)
            res.append(o if par == g else pltpu.roll(o, HEAD_DIM, 1))
        o_ref[:, j * LANES:(j + 1) * LANES] = jnp.where(lane < HEAD_DIM, res[0], res[1]).astype(BF16)


def _natten_body(q_ref, k_ref, v_ref, bias_ref, o_ref, *, rows):
    rb = pl.program_id(1)
    start = pl.multiple_of(jnp.clip(rb * NA_QROWS - NA_ROWS // 2, 0, rows - NA_KROWS) * GRID_W, GRID_W)
    lane = _lane_iota()
    nk = NA_KROWS * GRID_W
    for j in range(B_W // LANES):
        kk = k_ref[pl.ds(start, nk), j * LANES:(j + 1) * LANES]
        vv = v_ref[pl.ds(start, nk), j * LANES:(j + 1) * LANES]
        q2 = q_ref[:, j * LANES:(j + 1) * LANES].astype(F32)
        res = []
        for par in range(2):
            keep = (lane < HEAD_DIM) if par == 0 else (lane >= HEAD_DIM)
            qh = jnp.where(keep, q2, 0.0).astype(BF16)
            res.append(_softmax_pv(_dot_nt(qh, kk) + bias_ref[0, 2 * j + par], vv))
        o_ref[:, j * LANES:(j + 1) * LANES] = jnp.where(lane < HEAD_DIM, res[0], res[1]).astype(BF16)


def _diff_body(lam_ref, sg_ref, q_ref, k_ref, v_ref, o_ref, *, lam_init):
    lp = lam_ref[...]
    lam = (jnp.exp(jnp.sum(lp[0:1] * lp[1:2], axis=-1, keepdims=True))
           - jnp.exp(jnp.sum(lp[2:3] * lp[3:4], axis=-1, keepdims=True)) + lam_init)
    q = q_ref[...].astype(F32)
    k = k_ref[...]
    lane = _lane_iota()
    maps = []
    for m in range(2):
        keep = (lane < HEAD_DIM) if m == 0 else (lane >= HEAD_DIM)
        s = _dot_nt(jnp.where(keep, q, 0.0).astype(BF16), k)
        p = jnp.exp(s - jnp.max(s, axis=-1, keepdims=True))
        maps.append(p / jnp.sum(p, axis=-1, keepdims=True))
    a = (maps[0] - lam * maps[1]).astype(BF16)
    o = _rms(_dot(a, v_ref[...]), sg_ref[...]) * (1.0 - lam_init)
    o_ref[...] = o.astype(BF16)


def _gqa(a, seq):
    r0, nb, L = seq
    tq = ATT_Q
    nq = L // tq
    return pl.pallas_call(
        _gqa_body, grid=(nb, nq),
        in_specs=[pl.BlockSpec((tq, A_Q_W), lambda b, i: (r0 // tq + b * nq + i, 0)),
                  pl.BlockSpec((L, LANES), lambda b, i: (r0 // L + b, A_Q_W // LANES)),
                  pl.BlockSpec((L, LANES), lambda b, i: (r0 // L + b, A_Q_W // LANES + 1))],
        out_specs=pl.BlockSpec((tq, A_Q_W), lambda b, i: (b * nq + i, 0)),
        out_shape=jax.ShapeDtypeStruct((nb * L, A_Q_W), BF16),
        compiler_params=_params("parallel", "parallel"), name="gqa_axial")(a, a, a)


def _natten(bq, bias, seq):
    r0, nb, L = seq
    rows = L // GRID_W
    assert rows >= NA_KROWS and rows % NA_QROWS == 0
    tq = NA_QROWS * GRID_W
    nq = L // tq

    def bias_index(b, i):
        return (jnp.where(i == 0, 0, jnp.where(i == nq - 1, 2, 1)), 0, 0, 0)

    return pl.pallas_call(
        functools.partial(_natten_body, rows=rows), grid=(nb, nq),
        in_specs=[pl.BlockSpec((tq, B_W), lambda b, i: (r0 // tq + b * nq + i, 0)),
                  pl.BlockSpec((L, B_W), lambda b, i: (r0 // L + b, 1)),
                  pl.BlockSpec((L, B_W), lambda b, i: (r0 // L + b, 2)),
                  pl.BlockSpec((1,) + bias.shape[1:], bias_index)],
        out_specs=pl.BlockSpec((tq, B_W), lambda b, i: (b * nq + i, 0)),
        out_shape=jax.ShapeDtypeStruct((nb * L, B_W), BF16),
        compiler_params=_params("parallel", "arbitrary"), name="natten")(bq, bq, bq, bias)


def _diff(c, lam_rows, sub_gain, lam_init, seq):
    r0, nb, L = seq
    tq = ATT_Q
    nq = L // tq
    nh = D_MODEL // LANES
    fixed = lambda b, h, i: (0, 0)
    return pl.pallas_call(
        functools.partial(_diff_body, lam_init=lam_init), grid=(nb, nh, nq),
        in_specs=[pl.BlockSpec((8, LANES), fixed), pl.BlockSpec((1, LANES), fixed),
                  pl.BlockSpec((tq, LANES), lambda b, h, i: (r0 // tq + b * nq + i, h)),
                  pl.BlockSpec((L, LANES), lambda b, h, i: (r0 // L + b, nh + h)),
                  pl.BlockSpec((L, LANES), lambda b, h, i: (r0 // L + b, 2 * nh + h))],
        out_specs=pl.BlockSpec((tq, LANES), lambda b, h, i: (b * nq + i, h)),
        out_shape=jax.ShapeDtypeStruct((nb * L, D_MODEL), BF16),
        compiler_params=_params("parallel", "parallel", "parallel"), name="diff_attn",
    )(lam_rows, sub_gain, c, c, c)


def _route_body(lg_ref, meta_ref, cnt_ref, run_ref):
    @pl.when(pl.program_id(0) == 0)
    def _():
        run_ref[...] = jnp.zeros_like(run_ref)

    lg = lg_ref[...]
    tm = lg.shape[0]
    lane_i = _lane_iota(lg.shape)
    lane = lane_i.astype(F32)
    far = float(LANES)
    is_g = lane_i < N_GROUPS
    gl = jnp.where(is_g, lg, NEG)
    gmax = jnp.max(gl, axis=-1, keepdims=True)
    gidx = jnp.min(jnp.where(is_g & (gl == gmax), lane, far), axis=-1, keepdims=True)
    g_w = 1.0 / jnp.sum(jnp.where(is_g, jnp.exp(gl - gmax), 0.0), axis=-1, keepdims=True)
    eid_i = lane_i - N_GROUPS
    eid = eid_i.astype(F32)
    grp = lax.shift_right_arithmetic(eid_i, int(math.log2(EXPERTS_PER_GROUP))).astype(F32)
    in_grp = (eid_i >= 0) & (eid_i < N_EXPERTS) & (grp == gidx)
    el = jnp.where(in_grp, lg, NEG)
    e1 = jnp.max(el, axis=-1, keepdims=True)
    i1 = jnp.min(jnp.where(in_grp & (el == e1), eid, far), axis=-1, keepdims=True)
    rest = in_grp & (eid != i1)
    el2 = jnp.where(rest, lg, NEG)
    e2 = jnp.max(el2, axis=-1, keepdims=True)
    i2 = jnp.min(jnp.where(rest & (el2 == e2), eid, far), axis=-1, keepdims=True)
    t = jnp.exp(e2 - e1)
    w1 = g_w / (1.0 + t)
    w2 = g_w * t / (1.0 + t)
    pick = ((lane == i1) | (lane == i2))
    onehot = jnp.where(pick, 1.0, 0.0)
    r = lax.broadcasted_iota(I32, (tm, tm), 0)
    c = lax.broadcasted_iota(I32, (tm, tm), 1)
    before = jnp.where(c < r, 1.0, 0.0).astype(BF16)
    prefix = _dot(before, onehot.astype(BF16)) + run_ref[...]
    rank1 = jnp.sum(jnp.where(lane == i1, prefix, 0.0), axis=-1, keepdims=True)
    rank2 = jnp.sum(jnp.where(lane == i2, prefix, 0.0), axis=-1, keepdims=True)
    run = run_ref[...] + jnp.sum(onehot, axis=0, keepdims=True)
    run_ref[...] = run
    cnt_ref[...] = run
    meta = jnp.zeros(lg.shape, F32)
    for n, col in enumerate((i1, i2, w1, w2, rank1, rank2)):
        meta = jnp.where(lane_i == n, col, meta)
    meta_ref[...] = meta


def _route(logits):
    t = logits.shape[0]
    tm = ROUTE_TILE
    return pl.pallas_call(
        _route_body,
        grid=(t // tm,),
        in_specs=[pl.BlockSpec((tm, LANES), lambda i: (i, 0))],
        out_specs=[pl.BlockSpec((tm, LANES), lambda i: (i, 0)), pl.BlockSpec((1, LANES), lambda i: (0, 0))],
        out_shape=[jax.ShapeDtypeStruct((t, LANES), F32), jax.ShapeDtypeStruct((1, LANES), F32)],
        scratch_shapes=[pltpu.VMEM((1, LANES), F32)],
        compiler_params=_params("arbitrary"),
        name="route",
    )(logits)


def _row_copy(src, i, dst, j, sem):
    return pltpu.make_async_copy(src.at[pl.ds(i, 1), :], dst.at[pl.ds(j, 1), :], sem)


def _dispatch_body(dest_ref, hn_ref, xs_in_ref, xs_ref, sem):
    del xs_in_ref
    tm = hn_ref.shape[0]

    def issue(k, carry):
        _row_copy(hn_ref, k, xs_ref, dest_ref[0, 0, k], sem).start()
        _row_copy(hn_ref, k, xs_ref, dest_ref[0, 0, tm + k], sem).start()
        return carry

    def drain(k, carry):
        _row_copy(hn_ref, 0, xs_ref, 0, sem).wait()
        _row_copy(hn_ref, 0, xs_ref, 0, sem).wait()
        return carry

    lax.fori_loop(0, tm, issue, 0)
    lax.fori_loop(0, tm, drain, 0)


def _dispatch(hn, dest, n_rows):
    t = hn.shape[0]
    tm = ROUTE_TILE
    zeros = jnp.zeros((n_rows, D_MODEL), F32)
    return pl.pallas_call(
        _dispatch_body,
        grid=(t // tm,),
        in_specs=[pl.BlockSpec((1, 1, 2 * tm), lambda i: (i, 0, 0), memory_space=pltpu.SMEM),
                  pl.BlockSpec((tm, D_MODEL), lambda i: (i, 0)),
                  pl.BlockSpec(memory_space=pl.ANY)],
        out_specs=pl.BlockSpec(memory_space=pl.ANY),
        out_shape=jax.ShapeDtypeStruct((n_rows, D_MODEL), F32),
        scratch_shapes=[pltpu.SemaphoreType.DMA],
        input_output_aliases={2: 0},
        compiler_params=_params("arbitrary"),
        name="moe_dispatch",
    )(dest, hn, zeros)


def _expert_body(te_ref, used_ref, x_ref, wg_ref, wu_ref, wd_ref, y_ref):
    i = pl.program_id(0)

    @pl.when(i < used_ref[0])
    def _():
        xb = x_ref[...].astype(BF16)
        g = _dot(xb, wg_ref[0])
        u = _dot(xb, wu_ref[0])
        h = (g / (1.0 + jnp.exp(-g)) * u).astype(BF16)
        y_ref[...] = _dot(h, wd_ref[0])

    @pl.when(i >= used_ref[0])
    def _():
        y_ref[...] = jnp.zeros_like(y_ref)


def _experts(xs, tile_expert, n_used, w_gate, w_up, w_down):
    n_rows = xs.shape[0]
    tm = EXPERT_TILE
    grid_spec = pltpu.PrefetchScalarGridSpec(
        num_scalar_prefetch=2,
        grid=(n_rows // tm,),
        in_specs=[pl.BlockSpec((tm, D_MODEL), lambda i, te, nu: (i, 0)),
                  pl.BlockSpec((1, D_MODEL, D_EXPERT), lambda i, te, nu: (te[i], 0, 0)),
                  pl.BlockSpec((1, D_MODEL, D_EXPERT), lambda i, te, nu: (te[i], 0, 0)),
                  pl.BlockSpec((1, D_EXPERT, D_MODEL), lambda i, te, nu: (te[i], 0, 0))],
        out_specs=pl.BlockSpec((tm, D_MODEL), lambda i, te, nu: (i, 0)),
    )
    return pl.pallas_call(
        _expert_body,
        grid_spec=grid_spec,
        out_shape=jax.ShapeDtypeStruct((n_rows, D_MODEL), F32),
        compiler_params=_params("arbitrary"),
        name="moe_experts",
    )(tile_expert, n_used, xs, w_gate, w_up, w_down)


def _combine_body(dest_ref, x_ref, meta_ref, ys_ref, o_ref, ya_ref, yb_ref, sem):
    tm = x_ref.shape[0]

    def issue(k, carry):
        _row_copy(ys_ref, dest_ref[0, 0, k], ya_ref, k, sem).start()
        _row_copy(ys_ref, dest_ref[0, 0, tm + k], yb_ref, k, sem).start()
        return carry

    def drain(k, carry):
        _row_copy(ys_ref, 0, ya_ref, 0, sem).wait()
        _row_copy(ys_ref, 0, yb_ref, 0, sem).wait()
        return carry

    lax.fori_loop(0, tm, issue, 0)
    lax.fori_loop(0, tm, drain, 0)
    meta = meta_ref[...]
    o_ref[...] = x_ref[...] + meta[:, 2:3] * ya_ref[...] + meta[:, 3:4] * yb_ref[...]


def _combine(x1, meta, dest, ys):
    t = x1.shape[0]
    tm = ROUTE_TILE
    return pl.pallas_call(
        _combine_body,
        grid=(t // tm,),
        in_specs=[pl.BlockSpec((1, 1, 2 * tm), lambda i: (i, 0, 0), memory_space=pltpu.SMEM),
                  pl.BlockSpec((tm, D_MODEL), lambda i: (i, 0)),
                  pl.BlockSpec((tm, LANES), lambda i: (i, 0)),
                  pl.BlockSpec(memory_space=pl.ANY)],
        out_specs=pl.BlockSpec((tm, D_MODEL), lambda i: (i, 0)),
        out_shape=jax.ShapeDtypeStruct((t, D_MODEL), F32),
        scratch_shapes=[pltpu.VMEM((tm, D_MODEL), F32), pltpu.VMEM((tm, D_MODEL), F32),
                        pltpu.SemaphoreType.DMA],
        compiler_params=_params("arbitrary"),
        name="moe_combine",
    )(dest, x1, meta, ys)


def _moe(x1, hn, logits, w_gate, w_up, w_down):
    t = x1.shape[0]
    tm = ROUTE_TILE
    meta, counts = _route(logits)
    counts = counts[0, :N_EXPERTS].astype(I32)
    padded = (counts + EXPERT_TILE - 1) // EXPERT_TILE * EXPERT_TILE
    ends = jnp.cumsum(padded)
    offsets = ends - padded
    n_rows = 2 * t + N_EXPERTS * EXPERT_TILE
    n_tiles = n_rows // EXPERT_TILE
    n_used = (ends[-1] // EXPERT_TILE).astype(I32)
    tile_expert = jnp.searchsorted(ends, jnp.arange(n_tiles, dtype=I32) * EXPERT_TILE, side="right").astype(I32)
    last = jnp.take(tile_expert, jnp.maximum(n_used - 1, 0))
    tile_expert = jnp.where(jnp.arange(n_tiles) < n_used, tile_expert, last)
    e1 = meta[:, 0].astype(I32)
    e2 = meta[:, 1].astype(I32)
    dest1 = jnp.take(offsets, e1) + meta[:, 4].astype(I32)
    dest2 = jnp.take(offsets, e2) + meta[:, 5].astype(I32)
    dest = jnp.concatenate([dest1.reshape(t // tm, 1, tm), dest2.reshape(t // tm, 1, tm)], axis=-1)
    xs = _dispatch(hn, dest, n_rows)
    ys = _experts(xs, tile_expert, n_used.reshape(1), w_gate, w_up, w_down)
    return _combine(x1, meta, dest, ys)


def kernel(x_prompt, x_sample, norm_mix, norm_ffn, norm_final, w_in_even, q_gain, k_gain, rpb, w_out_even,
           w_in_odd, lam_q1, lam_k1, lam_q2, lam_k2, subln_gain, w_out_odd,
           w_rg, b_rg, w_re, b_re, w_gate, w_up, w_down):
    bp, lp, d = x_prompt.shape
    bs, ls, _ = x_sample.shape
    assert d == D_MODEL and (bp * lp) % ls == 0 and lp % TOK_TILE == 0 and ls % TOK_TILE == 0
    tp, ts = bp * lp, bs * ls
    t = tp + ts
    seqs = ((0, bp, lp), (tp, bs, ls))
    x = jnp.concatenate([x_prompt.reshape(tp, d), x_sample.reshape(ts, d)], axis=0)
    max_len = max(lp, ls)
    depth = norm_mix.shape[0]

    def router(i):
        w = jnp.zeros((D_MODEL, LANES), F32)
        w = w.at[:, :N_GROUPS].set(w_rg[i]).at[:, N_GROUPS:N_GROUPS + N_EXPERTS].set(w_re[i])
        b = jnp.zeros((1, LANES), F32)
        b = b.at[0, :N_GROUPS].set(b_rg[i]).at[0, N_GROUPS:N_GROUPS + N_EXPERTS].set(b_re[i])
        return w, b

    for i in range(depth):
        j = i // 2
        if i % 2 == 0:
            head_gain = jnp.stack([jnp.tile(q_gain[j], 2), jnp.tile(k_gain[j], 2)])
            a, b = _inproj(x, norm_mix[i], w_in_even[j].astype(BF16), _axial_tables(max_len), seqs, head_gain)
            bias = _natten_bias(rpb[j])
            pieces = [[_gqa(a, seq), _natten(b, bias, seq)] for seq in seqs]
            w_out = w_out_even[j]
        else:
            (c,) = _inproj(x, norm_mix[i], w_in_odd[j].astype(BF16), _partial_tables(max_len), seqs)
            lam_init = 0.8 - 0.6 * math.exp(-0.3 * i)
            lam_rows = jnp.zeros((8, LANES), F32).at[:4, :HEAD_DIM].set(
                jnp.stack([lam_q1[j], lam_k1[j], lam_q2[j], lam_k2[j]]))
            pieces = [[_diff(c, lam_rows, subln_gain[j].reshape(1, LANES), lam_init, seq)] for seq in seqs]
            w_out = w_out_odd[j]
        w_router, b_router = router(i)
        x1, hn, logits = _outproj(pieces, w_out.astype(BF16), x, norm_ffn[i], w_router, b_router)
        x = _moe(x1, hn, logits, w_gate[i].astype(BF16), w_up[i].astype(BF16), w_down[i].astype(BF16))
    y_prompt = _final_norm(x, norm_final, 0, tp).reshape(bp, lp, d)
    y_sample = _final_norm(x, norm_final, tp, ts).reshape(bs, ls, d)
    return (y_prompt, y_sample)
```

```python
import functools
import math

import numpy as np
import jax
import jax.numpy as jnp
from jax import lax
from jax.experimental import pallas as pl
from jax.experimental.pallas import tpu as pltpu

F32 = jnp.float32
BF16 = jnp.bfloat16
I32 = jnp.int32

D_MODEL = 1024
HEAD_DIM = 64
GRID_W = 64
EPS = 1e-6
LANES = 128
A_Q_W, A_KV_W, B_W = 512, 128, 512
EVEN_IN = A_Q_W + 2 * A_KV_W + 3 * B_W
ODD_IN = 3 * D_MODEL
AXIAL_THETA = 10000.0
ROPE_THETA = 500000.0
ROPE_DIMS = HEAD_DIM // 4
NA_ROWS, NA_COLS = 8, 16
NA_QROWS = 4
NA_KROWS = NA_QROWS + NA_ROWS
N_GROUPS, EXPERTS_PER_GROUP = 4, 8
N_EXPERTS = N_GROUPS * EXPERTS_PER_GROUP
D_EXPERT = 512
LOG2E = math.log2(math.e)
SCALE = HEAD_DIM ** -0.5 * LOG2E
NEG = -1e30

VMEM_LIMIT_BYTES = 56 * 1024 * 1024
TOK_TILE = 512
ROUTE_TILE = 256
MOVE_TILE = 1024
MOVE_UNROLL = 8
EXPERT_TILE = 512
ATT_Q = 256


def _params(*sem):
    return pltpu.CompilerParams(dimension_semantics=sem, vmem_limit_bytes=VMEM_LIMIT_BYTES)


def _lane_iota(shape=(1, LANES)):
    return lax.broadcasted_iota(I32, shape, len(shape) - 1)


def _rms(x, gain):
    return x * lax.rsqrt(jnp.mean(x * x, axis=-1, keepdims=True) + EPS) * gain


def _dot(a, b):
    return jnp.dot(a, b, preferred_element_type=F32)


def _dot_nt(a, b):
    return lax.dot_general(a, b, (((1,), (1,)), ((), ())), preferred_element_type=F32)


def _rope(y, tab_ref, shift):
    return (y * tab_ref[0] + pltpu.roll(y, LANES - shift, 1) * tab_ref[1]
            + pltpu.roll(y, shift, 1) * tab_ref[2])


def _rope_tables(angles, n):
    cos = jnp.cos(angles)
    sin = jnp.sin(angles)
    low = jnp.asarray((np.arange(HEAD_DIM) % n) < n // 2)
    tab = jnp.stack([cos, jnp.where(low, -sin, 0.0), jnp.where(low, 0.0, sin)])
    return jnp.concatenate([tab, tab], axis=-1).astype(F32)


def _axial_tables(max_len):
    pos = jnp.arange(max_len)
    row = (pos // GRID_W).astype(F32)
    col = (pos % GRID_W).astype(F32)
    half = HEAD_DIM // 2
    inv = AXIAL_THETA ** (-jnp.arange(0, half, 2, dtype=F32) / half)
    inv2 = jnp.concatenate([inv, inv])
    ang = jnp.concatenate([row[:, None] * inv2[None], col[:, None] * inv2[None]], axis=-1)
    return _rope_tables(ang, half)


def _partial_tables(max_len):
    pos = jnp.arange(max_len).astype(F32)
    inv = ROPE_THETA ** (-jnp.arange(0, ROPE_DIMS, 2, dtype=F32) / ROPE_DIMS)
    inv2 = jnp.concatenate([inv, inv])
    ang = pos[:, None] * inv2[None]
    tab = _rope_tables(jnp.concatenate([ang, jnp.zeros((max_len, HEAD_DIM - ROPE_DIMS), F32)], -1), ROPE_DIMS)
    keep = jnp.asarray(np.tile(np.arange(HEAD_DIM) < ROPE_DIMS, 2))
    return jnp.stack([jnp.where(keep, tab[0], 1.0), jnp.where(keep, tab[1], 0.0), jnp.where(keep, tab[2], 0.0)])


def _natten_bias(rpb):
    c, kc = np.arange(GRID_W)[:, None], np.arange(GRID_W)[None, :]
    cs = np.clip(c - NA_COLS // 2, 0, GRID_W - NA_COLS)
    col_ok = (kc >= cs) & (kc < cs + NA_COLS)
    col_pick = (kc - c + NA_COLS - 1)[None] == np.arange(2 * NA_COLS - 1)[:, None, None]
    by_col = jnp.einsum("hdm,mcq->hdcq", rpb.astype(F32), jnp.asarray(col_pick, F32),
                        precision=lax.Precision.HIGHEST)
    rl, ki = np.arange(NA_QROWS)[:, None], np.arange(NA_KROWS)[None, :]
    out = []
    for delta, first in ((0, 0 * rl), (NA_QROWS, rl), (2 * NA_QROWS, 0 * rl + NA_QROWS)):
        row_ok = (ki >= first) & (ki < first + NA_ROWS)
        row_pick = ((ki - delta - rl + NA_ROWS - 1)[None] == np.arange(2 * NA_ROWS - 1)[:, None, None]) & row_ok
        b = jnp.einsum("drk,hdcq->hrckq", jnp.asarray(row_pick, F32), by_col, precision=lax.Precision.HIGHEST)
        ok = row_ok[:, None, :, None] & col_ok[None, :, None, :]
        b = jnp.where(jnp.asarray(ok)[None], b * LOG2E, NEG)
        out.append(b.reshape(rpb.shape[0], NA_QROWS * GRID_W, NA_KROWS * GRID_W))
    return jnp.stack(out)


def _inproj_even_body(x_ref, g_ref, w_ref, tab_ref, hg_ref, oa_ref, ob_ref):
    hn = _rms(x_ref[...], g_ref[...]).astype(BF16)
    low = _lane_iota() < HEAD_DIM

    def head_norm_rope(y, gain, scale):
        ss = y * y
        s_lo = jnp.sum(jnp.where(low, ss, 0.0), axis=-1, keepdims=True)
        s_hi = jnp.sum(jnp.where(low, 0.0, ss), axis=-1, keepdims=True)
        y = y * lax.rsqrt(jnp.where(low, s_lo, s_hi) * (1.0 / HEAD_DIM) + EPS) * gain
        return _rope(y, tab_ref, HEAD_DIM // 4) * scale

    for j in range(A_Q_W // LANES):
        c = j * LANES
        oa_ref[:, c:c + LANES] = head_norm_rope(_dot(hn, w_ref[:, c:c + LANES]), hg_ref[0:1, :], SCALE).astype(BF16)
    c = A_Q_W
    oa_ref[:, c:c + LANES] = head_norm_rope(_dot(hn, w_ref[:, c:c + LANES]), hg_ref[1:2, :], 1.0).astype(BF16)
    c = A_Q_W + A_KV_W
    oa_ref[:, c:c + LANES] = _dot(hn, w_ref[:, c:c + LANES]).astype(BF16)
    c0 = A_Q_W + 2 * A_KV_W
    for j in range(3 * B_W // 256):
        c = j * 256
        y = _dot(hn, w_ref[:, c0 + c:c0 + c + 256])
        ob_ref[:, c:c + 256] = (y * SCALE if c < B_W else y).astype(BF16)


def _inproj_odd_body(x_ref, g_ref, w_ref, tab_ref, oc_ref):
    hn = _rms(x_ref[...], g_ref[...]).astype(BF16)
    for j in range(2 * D_MODEL // LANES):
        c = j * LANES
        y = _rope(_dot(hn, w_ref[:, c:c + LANES]), tab_ref, ROPE_DIMS // 2)
        oc_ref[:, c:c + LANES] = (y * SCALE if c < D_MODEL else y).astype(BF16)
    for j in range(D_MODEL // 256):
        c = 2 * D_MODEL + j * 256
        oc_ref[:, c:c + 256] = _dot(hn, w_ref[:, c:c + 256]).astype(BF16)


def _tab_index(seqs, tm):
    (r0, _, l0), (r1, _, l1) = seqs

    def index(i):
        return (0, jnp.where(i < r1 // tm, (i - r0 // tm) % (l0 // tm), (i - r1 // tm) % (l1 // tm)), 0)
    return index


def _inproj(x, gain, w, tab, seqs, head_gain=None):
    t = x.shape[0]
    tm = TOK_TILE
    even = head_gain is not None
    in_specs = [pl.BlockSpec((tm, D_MODEL), lambda i: (i, 0)),
                pl.BlockSpec((1, D_MODEL), lambda i: (0, 0)),
                pl.BlockSpec(w.shape, lambda i: (0, 0)),
                pl.BlockSpec((3, tm, LANES), _tab_index(seqs, tm))]
    args = [x, gain.reshape(1, D_MODEL), w, tab]
    if even:
        in_specs.append(pl.BlockSpec((2, LANES), lambda i: (0, 0)))
        args.append(head_gain)
        widths = (A_Q_W + 2 * A_KV_W, 3 * B_W)
        body = _inproj_even_body
    else:
        widths = (ODD_IN,)
        body = _inproj_odd_body
    out = pl.pallas_call(
        body,
        grid=(t // tm,),
        in_specs=in_specs,
        out_specs=[pl.BlockSpec((tm, n), lambda i: (i, 0)) for n in widths],
        out_shape=[jax.ShapeDtypeStruct((t, n), BF16) for n in widths],
        compiler_params=_params("parallel"),
        name="inproj_even" if even else "inproj_odd",
    )(*args)
    return out


def _outproj_body(*refs, n_pieces, first_tiles):
    o0, o1 = refs[:n_pieces], refs[n_pieces:2 * n_pieces]
    w_ref, x_ref, g_ref, wr_ref, br_ref, x1_ref, hn_ref, lg_ref = refs[2 * n_pieces:]
    in_first = pl.program_id(0) < first_tiles
    x1 = x_ref[...]
    c = 0
    for a_ref, b_ref in zip(o0, o1):
        n = a_ref.shape[1]
        x1 = x1 + _dot(jnp.where(in_first, a_ref[...], b_ref[...]), w_ref[c:c + n, :])
        c += n
    x1_ref[...] = x1
    hn = _rms(x1, g_ref[...])
    hn_ref[...] = hn
    lg_ref[...] = jnp.dot(hn, wr_ref[...], precision=lax.Precision.HIGHEST,
                          preferred_element_type=F32) + br_ref[...]


def _outproj(pieces, w, x, gain, w_router, b_router):
    t = x.shape[0]
    tm = TOK_TILE
    row = lambda i: (i, 0)
    fixed = lambda i: (0, 0)
    first_tiles = pieces[0][0].shape[0] // tm
    piece_specs = (
        [pl.BlockSpec((tm, p.shape[1]), lambda i: (jnp.minimum(i, first_tiles - 1), 0)) for p in pieces[0]]
        + [pl.BlockSpec((tm, p.shape[1]), lambda i: (jnp.maximum(i - first_tiles, 0), 0)) for p in pieces[1]])
    return pl.pallas_call(
        functools.partial(_outproj_body, n_pieces=len(pieces[0]), first_tiles=first_tiles),
        grid=(t // tm,),
        in_specs=piece_specs + [
            pl.BlockSpec((D_MODEL, D_MODEL), fixed),
            pl.BlockSpec((tm, D_MODEL), row), pl.BlockSpec((1, D_MODEL), fixed),
            pl.BlockSpec((D_MODEL, LANES), fixed), pl.BlockSpec((1, LANES), fixed)],
        out_specs=[pl.BlockSpec((tm, D_MODEL), row), pl.BlockSpec((tm, D_MODEL), row),
                   pl.BlockSpec((tm, LANES), row)],
        out_shape=[jax.ShapeDtypeStruct((t, D_MODEL), F32), jax.ShapeDtypeStruct((t, D_MODEL), F32),
                   jax.ShapeDtypeStruct((t, LANES), F32)],
        compiler_params=_params("parallel"),
        name="outproj_router",
    )(*pieces[0], *pieces[1], w, x, gain.reshape(1, D_MODEL), w_router, b_router)


def _final_norm_body(x_ref, g_ref, o_ref):
    o_ref[...] = _rms(x_ref[...], g_ref[...])


def _final_norm(x, gain, r0, rows):
    tm = TOK_TILE
    return pl.pallas_call(
        _final_norm_body,
        grid=(rows // tm,),
        in_specs=[pl.BlockSpec((tm, D_MODEL), lambda i: (i + r0 // tm, 0)),
                  pl.BlockSpec((1, D_MODEL), lambda i: (0, 0))],
        out_specs=pl.BlockSpec((tm, D_MODEL), lambda i: (i, 0)),
        out_shape=jax.ShapeDtypeStruct((rows, D_MODEL), F32),
        compiler_params=_params("parallel"),
        name="final_norm",
    )(x, gain.reshape(1, D_MODEL))


def _softmax_pv(s, v):
    m = jnp.max(s, axis=-1, keepdims=True)
    p = jnp.exp2(s - m)
    l = jnp.sum(p, axis=-1, keepdims=True)
    return _dot(p.astype(BF16), v) / l


def _gqa_body(q_ref, k_ref, v_ref, o_ref):
    k = k_ref[...]
    v = v_ref[...]
    lane = _lane_iota()
    for j in range(A_Q_W // LANES):
        g = j // 2
        keep = (lane >= g * HEAD_DIM) & (lane < (g + 1) * HEAD_DIM)
        q2 = q_ref[:, j * LANES:(j + 1) * LANES].astype(F32)
        res = []
        for par in range(2):
            qh = q2 if par == g else pltpu.roll(q2, HEAD_DIM, 1)
            qh = jnp.where(keep, qh, 0.0).astype(BF16)
            o = _softmax_pv(_dot_nt(qh, k), v)
            res.append(o if par == g else pltpu.roll(o, HEAD_DIM, 1))
        o_ref[:, j * LANES:(j + 1) * LANES] = jnp.where(lane < HEAD_DIM, res[0], res[1]).astype(BF16)


def _natten_body(q_ref, k_ref, v_ref, bias_ref, o_ref, *, rows):
    rb = pl.program_id(1)
    start = pl.multiple_of(jnp.clip(rb * NA_QROWS - NA_ROWS // 2, 0, rows - NA_KROWS) * GRID_W, GRID_W)
    lane = _lane_iota()
    nk = NA_KROWS * GRID_W
    for j in range(B_W // LANES):
        kk = k_ref[pl.ds(start, nk), j * LANES:(j + 1) * LANES]
        vv = v_ref[pl.ds(start, nk), j * LANES:(j + 1) * LANES]
        q2 = q_ref[:, j * LANES:(j + 1) * LANES].astype(F32)
        res = []
        for par in range(2):
            keep = (lane < HEAD_DIM) if par == 0 else (lane >= HEAD_DIM)
            qh = jnp.where(keep, q2, 0.0).astype(BF16)
            res.append(_softmax_pv(_dot_nt(qh, kk) + bias_ref[0, 2 * j + par], vv))
        o_ref[:, j * LANES:(j + 1) * LANES] = jnp.where(lane < HEAD_DIM, res[0], res[1]).astype(BF16)


def _diff_body(lam_ref, sg_ref, q_ref, k_ref, v_ref, o_ref, *, lam_init):
    lp = lam_ref[...]
    lam = (jnp.exp(jnp.sum(lp[0:1] * lp[1:2], axis=-1, keepdims=True))
           - jnp.exp(jnp.sum(lp[2:3] * lp[3:4], axis=-1, keepdims=True)) + lam_init)
    q = q_ref[...].astype(F32)
    k = k_ref[...]
    v = v_ref[...]
    lane = _lane_iota()
    o0 = _softmax_pv(_dot_nt(jnp.where(lane < HEAD_DIM, q, 0.0).astype(BF16), k), v)
    o1 = _softmax_pv(_dot_nt(jnp.where(lane < HEAD_DIM, 0.0, q).astype(BF16), k), v)
    o = _rms(o0 - lam * o1, sg_ref[...]) * (1.0 - lam_init)
    o_ref[...] = o.astype(BF16)


def _gqa(a, seq):
    r0, nb, L = seq
    tq = ATT_Q
    nq = L // tq
    return pl.pallas_call(
        _gqa_body, grid=(nb, nq),
        in_specs=[pl.BlockSpec((tq, A_Q_W), lambda b, i: (r0 // tq + b * nq + i, 0)),
                  pl.BlockSpec((L, LANES), lambda b, i: (r0 // L + b, A_Q_W // LANES)),
                  pl.BlockSpec((L, LANES), lambda b, i: (r0 // L + b, A_Q_W // LANES + 1))],
        out_specs=pl.BlockSpec((tq, A_Q_W), lambda b, i: (b * nq + i, 0)),
        out_shape=jax.ShapeDtypeStruct((nb * L, A_Q_W), BF16),
        compiler_params=_params("parallel", "parallel"), name="gqa_axial")(a, a, a)


def _natten(bq, bias, seq):
    r0, nb, L = seq
    rows = L // GRID_W
    assert rows >= NA_KROWS and rows % NA_QROWS == 0
    tq = NA_QROWS * GRID_W
    nq = L // tq

    def bias_index(b, i):
        return (jnp.where(i == 0, 0, jnp.where(i == nq - 1, 2, 1)), 0, 0, 0)

    return pl.pallas_call(
        functools.partial(_natten_body, rows=rows), grid=(nb, nq),
        in_specs=[pl.BlockSpec((tq, B_W), lambda b, i: (r0 // tq + b * nq + i, 0)),
                  pl.BlockSpec((L, B_W), lambda b, i: (r0 // L + b, 1)),
                  pl.BlockSpec((L, B_W), lambda b, i: (r0 // L + b, 2)),
                  pl.BlockSpec((1,) + bias.shape[1:], bias_index)],
        out_specs=pl.BlockSpec((tq, B_W), lambda b, i: (b * nq + i, 0)),
        out_shape=jax.ShapeDtypeStruct((nb * L, B_W), BF16),
        compiler_params=_params("parallel", "arbitrary"), name="natten")(bq, bq, bq, bias)


def _diff(c, lam_rows, sub_gain, lam_init, seq):
    r0, nb, L = seq
    tq = ATT_Q
    nq = L // tq
    nh = D_MODEL // LANES
    fixed = lambda b, h, i: (0, 0)
    return pl.pallas_call(
        functools.partial(_diff_body, lam_init=lam_init), grid=(nb, nh, nq),
        in_specs=[pl.BlockSpec((8, LANES), fixed), pl.BlockSpec((1, LANES), fixed),
                  pl.BlockSpec((tq, LANES), lambda b, h, i: (r0 // tq + b * nq + i, h)),
                  pl.BlockSpec((L, LANES), lambda b, h, i: (r0 // L + b, nh + h)),
                  pl.BlockSpec((L, LANES), lambda b, h, i: (r0 // L + b, 2 * nh + h))],
        out_specs=pl.BlockSpec((tq, LANES), lambda b, h, i: (b * nq + i, h)),
        out_shape=jax.ShapeDtypeStruct((nb * L, D_MODEL), BF16),
        compiler_params=_params("parallel", "parallel", "parallel"), name="diff_attn",
    )(lam_rows, sub_gain, c, c, c)


def _route_body(lg_ref, meta_ref, cnt_ref, run_ref):
    @pl.when(pl.program_id(0) == 0)
    def _():
        run_ref[...] = jnp.zeros_like(run_ref)

    lg = lg_ref[...]
    tm = lg.shape[0]
    lane_i = _lane_iota(lg.shape)
    lane = lane_i.astype(F32)
    far = float(LANES)
    is_g = lane_i < N_GROUPS
    gl = jnp.where(is_g, lg, NEG)
    gmax = jnp.max(gl, axis=-1, keepdims=True)
    gidx = jnp.min(jnp.where(is_g & (gl == gmax), lane, far), axis=-1, keepdims=True)
    g_w = 1.0 / jnp.sum(jnp.where(is_g, jnp.exp(gl - gmax), 0.0), axis=-1, keepdims=True)
    eid_i = lane_i - N_GROUPS
    eid = eid_i.astype(F32)
    grp = lax.shift_right_arithmetic(eid_i, int(math.log2(EXPERTS_PER_GROUP))).astype(F32)
    in_grp = (eid_i >= 0) & (eid_i < N_EXPERTS) & (grp == gidx)
    el = jnp.where(in_grp, lg, NEG)
    e1 = jnp.max(el, axis=-1, keepdims=True)
    i1 = jnp.min(jnp.where(in_grp & (el == e1), eid, far), axis=-1, keepdims=True)
    rest = in_grp & (eid != i1)
    el2 = jnp.where(rest, lg, NEG)
    e2 = jnp.max(el2, axis=-1, keepdims=True)
    i2 = jnp.min(jnp.where(rest & (el2 == e2), eid, far), axis=-1, keepdims=True)
    t = jnp.exp(e2 - e1)
    w1 = g_w / (1.0 + t)
    w2 = g_w * t / (1.0 + t)
    pick = ((lane == i1) | (lane == i2))
    onehot = jnp.where(pick, 1.0, 0.0)
    r = lax.broadcasted_iota(I32, (tm, tm), 0)
    c = lax.broadcasted_iota(I32, (tm, tm), 1)
    before = jnp.where(c < r, 1.0, 0.0).astype(BF16)
    prefix = _dot(before, onehot.astype(BF16)) + run_ref[...]
    rank1 = jnp.sum(jnp.where(lane == i1, prefix, 0.0), axis=-1, keepdims=True)
    rank2 = jnp.sum(jnp.where(lane == i2, prefix, 0.0), axis=-1, keepdims=True)
    run = run_ref[...] + jnp.sum(onehot, axis=0, keepdims=True)
    run_ref[...] = run
    cnt_ref[...] = run
    meta = jnp.zeros(lg.shape, F32)
    for n, col in enumerate((i1, i2, w1, w2, rank1, rank2)):
        meta = jnp.where(lane_i == n, col, meta)
    meta_ref[...] = meta


def _route(logits):
    t = logits.shape[0]
    tm = ROUTE_TILE
    return pl.pallas_call(
        _route_body,
        grid=(t // tm,),
        in_specs=[pl.BlockSpec((tm, LANES), lambda i: (i, 0))],
        out_specs=[pl.BlockSpec((tm, LANES), lambda i: (i, 0)), pl.BlockSpec((1, LANES), lambda i: (0, 0))],
        out_shape=[jax.ShapeDtypeStruct((t, LANES), F32), jax.ShapeDtypeStruct((1, LANES), F32)],
        scratch_shapes=[pltpu.VMEM((1, LANES), F32)],
        compiler_params=_params("arbitrary"),
        name="route",
    )(logits)


def _row_copy(src, i, dst, j, sem):
    return pltpu.make_async_copy(src.at[pl.ds(i, 1), :], dst.at[pl.ds(j, 1), :], sem)


def _dispatch_body(dest_ref, hn_ref, xs_in_ref, xs_ref, sem):
    del xs_in_ref
    tm = hn_ref.shape[0]

    def issue(k0, carry):
        for u in range(MOVE_UNROLL):
            k = k0 * MOVE_UNROLL + u
            _row_copy(hn_ref, k, xs_ref, dest_ref[0, 0, k], sem).start()
            _row_copy(hn_ref, k, xs_ref, dest_ref[0, 0, tm + k], sem).start()
        return carry

    lax.fori_loop(0, tm // MOVE_UNROLL, issue, 0)
    for _ in range(2):
        pltpu.make_async_copy(hn_ref, xs_ref.at[pl.ds(0, tm), :], sem).wait()


def _dispatch(hn, dest, n_rows):
    t = hn.shape[0]
    tm = MOVE_TILE
    zeros = jnp.zeros((n_rows, D_MODEL), F32)
    return pl.pallas_call(
        _dispatch_body,
        grid=(t // tm,),
        in_specs=[pl.BlockSpec((1, 1, 2 * tm), lambda i: (i, 0, 0), memory_space=pltpu.SMEM),
                  pl.BlockSpec((tm, D_MODEL), lambda i: (i, 0)),
                  pl.BlockSpec(memory_space=pl.ANY)],
        out_specs=pl.BlockSpec(memory_space=pl.ANY),
        out_shape=jax.ShapeDtypeStruct((n_rows, D_MODEL), F32),
        scratch_shapes=[pltpu.SemaphoreType.DMA],
        input_output_aliases={2: 0},
        compiler_params=_params("arbitrary"),
        name="moe_dispatch",
    )(dest, hn, zeros)


def _expert_body(te_ref, used_ref, x_ref, wg_ref, wu_ref, wd_ref, y_ref):
    i = pl.program_id(0)

    @pl.when(i < used_ref[0])
    def _():
        xb = x_ref[...].astype(BF16)
        g = _dot(xb, wg_ref[0])
        u = _dot(xb, wu_ref[0])
        h = (g / (1.0 + jnp.exp(-g)) * u).astype(BF16)
        y_ref[...] = _dot(h, wd_ref[0])

    @pl.when(i >= used_ref[0])
    def _():
        y_ref[...] = jnp.zeros_like(y_ref)


def _experts(xs, tile_expert, n_used, w_gate, w_up, w_down):
    n_rows = xs.shape[0]
    tm = EXPERT_TILE
    grid_spec = pltpu.PrefetchScalarGridSpec(
        num_scalar_prefetch=2,
        grid=(n_rows // tm,),
        in_specs=[pl.BlockSpec((tm, D_MODEL), lambda i, te, nu: (i, 0)),
                  pl.BlockSpec((1, D_MODEL, D_EXPERT), lambda i, te, nu: (te[i], 0, 0)),
                  pl.BlockSpec((1, D_MODEL, D_EXPERT), lambda i, te, nu: (te[i], 0, 0)),
                  pl.BlockSpec((1, D_EXPERT, D_MODEL), lambda i, te, nu: (te[i], 0, 0))],
        out_specs=pl.BlockSpec((tm, D_MODEL), lambda i, te, nu: (i, 0)),
    )
    return pl.pallas_call(
        _expert_body,
        grid_spec=grid_spec,
        out_shape=jax.ShapeDtypeStruct((n_rows, D_MODEL), F32),
        compiler_params=_params("arbitrary"),
        name="moe_experts",
    )(tile_expert, n_used, xs, w_gate, w_up, w_down)


def _combine_body(dest_ref, x_ref, meta_ref, ys_ref, o_ref, ya_ref, yb_ref, sem):
    tm = x_ref.shape[0]

    def issue(k0, carry):
        for u in range(MOVE_UNROLL):
            k = k0 * MOVE_UNROLL + u
            _row_copy(ys_ref, dest_ref[0, 0, k], ya_ref, k, sem).start()
            _row_copy(ys_ref, dest_ref[0, 0, tm + k], yb_ref, k, sem).start()
        return carry

    lax.fori_loop(0, tm // MOVE_UNROLL, issue, 0)
    for buf in (ya_ref, yb_ref):
        pltpu.make_async_copy(ys_ref.at[pl.ds(0, tm), :], buf, sem).wait()
    meta = meta_ref[...]
    o_ref[...] = x_ref[...] + meta[:, 2:3] * ya_ref[...] + meta[:, 3:4] * yb_ref[...]


def _combine(x1, meta, dest, ys):
    t = x1.shape[0]
    tm = MOVE_TILE
    return pl.pallas_call(
        _combine_body,
        grid=(t // tm,),
        in_specs=[pl.BlockSpec((1, 1, 2 * tm), lambda i: (i, 0, 0), memory_space=pltpu.SMEM),
                  pl.BlockSpec((tm, D_MODEL), lambda i: (i, 0)),
                  pl.BlockSpec((tm, LANES), lambda i: (i, 0)),
                  pl.BlockSpec(memory_space=pl.ANY)],
        out_specs=pl.BlockSpec((tm, D_MODEL), lambda i: (i, 0)),
        out_shape=jax.ShapeDtypeStruct((t, D_MODEL), F32),
        scratch_shapes=[pltpu.VMEM((tm, D_MODEL), F32), pltpu.VMEM((tm, D_MODEL), F32),
                        pltpu.SemaphoreType.DMA],
        compiler_params=_params("arbitrary"),
        name="moe_combine",
    )(dest, x1, meta, ys)


def _moe(x1, hn, logits, w_gate, w_up, w_down):
    t = x1.shape[0]
    tm = MOVE_TILE
    meta, counts = _route(logits)
    counts = counts[0, :N_EXPERTS].astype(I32)
    padded = (counts + EXPERT_TILE - 1) // EXPERT_TILE * EXPERT_TILE
    ends = jnp.cumsum(padded)
    offsets = ends - padded
    n_rows = 2 * t + N_EXPERTS * EXPERT_TILE
    n_tiles = n_rows // EXPERT_TILE
    n_used = (ends[-1] // EXPERT_TILE).astype(I32)
    tile_start = jnp.minimum(jnp.arange(n_tiles, dtype=I32), n_used - 1) * EXPERT_TILE
    tile_expert = jnp.sum((ends[None, :] <= tile_start[:, None]).astype(I32), axis=1)
    experts = jnp.arange(N_EXPERTS, dtype=I32)[None, :]

    def position(e_col, rank_col):
        e = meta[:, e_col].astype(I32)[:, None]
        return jnp.sum(jnp.where(e == experts, offsets[None, :], 0), axis=1) + meta[:, rank_col].astype(I32)

    dest = jnp.concatenate([position(0, 4).reshape(t // tm, 1, tm), position(1, 5).reshape(t // tm, 1, tm)], axis=-1)
    xs = _dispatch(hn, dest, n_rows)
    ys = _experts(xs, tile_expert, n_used.reshape(1), w_gate, w_up, w_down)
    return _combine(x1, meta, dest, ys)


def kernel(x_prompt, x_sample, norm_mix, norm_ffn, norm_final, w_in_even, q_gain, k_gain, rpb, w_out_even,
           w_in_odd, lam_q1, lam_k1, lam_q2, lam_k2, subln_gain, w_out_odd,
           w_rg, b_rg, w_re, b_re, w_gate, w_up, w_down):
    bp, lp, d = x_prompt.shape
    bs, ls, _ = x_sample.shape
    assert d == D_MODEL and (bp * lp) % ls == 0 and lp % TOK_TILE == 0 and ls % TOK_TILE == 0
    tp, ts = bp * lp, bs * ls
    t = tp + ts
    seqs = ((0, bp, lp), (tp, bs, ls))
    x = jnp.concatenate([x_prompt.reshape(tp, d), x_sample.reshape(ts, d)], axis=0)
    max_len = max(lp, ls)
    depth = norm_mix.shape[0]

    def router(i):
        w = jnp.zeros((D_MODEL, LANES), F32)
        w = w.at[:, :N_GROUPS].set(w_rg[i]).at[:, N_GROUPS:N_GROUPS + N_EXPERTS].set(w_re[i])
        b = jnp.zeros((1, LANES), F32)
        b = b.at[0, :N_GROUPS].set(b_rg[i]).at[0, N_GROUPS:N_GROUPS + N_EXPERTS].set(b_re[i])
        return w, b

    for i in range(depth):
        j = i // 2
        if i % 2 == 0:
            head_gain = jnp.stack([jnp.tile(q_gain[j], 2), jnp.tile(k_gain[j], 2)])
            a, b = _inproj(x, norm_mix[i], w_in_even[j].astype(BF16), _axial_tables(max_len), seqs, head_gain)
            bias = _natten_bias(rpb[j])
            pieces = [[_gqa(a, seq), _natten(b, bias, seq)] for seq in seqs]
            w_out = w_out_even[j]
        else:
            (c,) = _inproj(x, norm_mix[i], w_in_odd[j].astype(BF16), _partial_tables(max_len), seqs)
            lam_init = 0.8 - 0.6 * math.exp(-0.3 * i)
            lam_rows = jnp.zeros((8, LANES), F32).at[:4, :HEAD_DIM].set(
                jnp.stack([lam_q1[j], lam_k1[j], lam_q2[j], lam_k2[j]]))
            pieces = [[_diff(c, lam_rows, subln_gain[j].reshape(1, LANES), lam_init, seq)] for seq in seqs]
            w_out = w_out_odd[j]
        w_router, b_router = router(i)
        x1, hn, logits = _outproj(pieces, w_out.astype(BF16), x, norm_ffn[i], w_router, b_router)
        x = _moe(x1, hn, logits, w_gate[i].astype(BF16), w_up[i].astype(BF16), w_down[i].astype(BF16))
    y_prompt = _final_norm(x, norm_final, 0, tp).reshape(bp, lp, d)
    y_sample = _final_norm(x, norm_final, tp, ts).reshape(bs, ls, d)
    return (y_prompt, y_sample)
```

```python
import functools
import math

import numpy as np
import jax
import jax.numpy as jnp
from jax import lax
from jax.experimental import pallas as pl
from jax.experimental.pallas import tpu as pltpu

F32 = jnp.float32
BF16 = jnp.bfloat16
I32 = jnp.int32

D_MODEL = 1024
HEAD_DIM = 64
GRID_W = 64
EPS = 1e-6
LANES = 128
A_Q_W, A_KV_W, B_W = 512, 128, 512
EVEN_IN = A_Q_W + 2 * A_KV_W + 3 * B_W
ODD_IN = 3 * D_MODEL
AXIAL_THETA = 10000.0
ROPE_THETA = 500000.0
ROPE_DIMS = HEAD_DIM // 4
NA_ROWS, NA_COLS = 8, 16
NA_QROWS = 4
NA_KROWS = NA_QROWS + NA_ROWS
N_GROUPS, EXPERTS_PER_GROUP = 4, 8
N_EXPERTS = N_GROUPS * EXPERTS_PER_GROUP
D_EXPERT = 512
LOG2E = math.log2(math.e)
SCALE = HEAD_DIM ** -0.5 * LOG2E
NEG = -1e30

VMEM_LIMIT_BYTES = 56 * 1024 * 1024
TOK_TILE = 512
ROUTE_TILE = 512
MOVE_TILE = 1024
MOVE_UNROLL = 8
EXPERT_TILE = 512
ATT_Q = 256
DIFF_HEADS = 2
KEY_CHUNK = 4096


def _params(*sem):
    return pltpu.CompilerParams(dimension_semantics=sem, vmem_limit_bytes=VMEM_LIMIT_BYTES)


def _lane_iota(shape=(1, LANES)):
    return lax.broadcasted_iota(I32, shape, len(shape) - 1)


def _rms(x, gain):
    return x * lax.rsqrt(jnp.mean(x * x, axis=-1, keepdims=True) + EPS) * gain


def _dot(a, b):
    return jnp.dot(a, b, preferred_element_type=F32)


def _dot_nt(a, b):
    return lax.dot_general(a, b, (((1,), (1,)), ((), ())), preferred_element_type=F32)


def _rope(y, tab_ref, shift):
    return (y * tab_ref[0] + pltpu.roll(y, LANES - shift, 1) * tab_ref[1]
            + pltpu.roll(y, shift, 1) * tab_ref[2])


def _rope_tables(angles, n):
    cos = jnp.cos(angles)
    sin = jnp.sin(angles)
    low = jnp.asarray((np.arange(HEAD_DIM) % n) < n // 2)
    tab = jnp.stack([cos, jnp.where(low, -sin, 0.0), jnp.where(low, 0.0, sin)])
    return jnp.concatenate([tab, tab], axis=-1).astype(F32)


def _axial_tables(max_len):
    pos = jnp.arange(max_len)
    row = (pos // GRID_W).astype(F32)
    col = (pos % GRID_W).astype(F32)
    half = HEAD_DIM // 2
    inv = AXIAL_THETA ** (-jnp.arange(0, half, 2, dtype=F32) / half)
    inv2 = jnp.concatenate([inv, inv])
    ang = jnp.concatenate([row[:, None] * inv2[None], col[:, None] * inv2[None]], axis=-1)
    return _rope_tables(ang, half)


def _partial_tables(max_len):
    pos = jnp.arange(max_len).astype(F32)
    inv = ROPE_THETA ** (-jnp.arange(0, ROPE_DIMS, 2, dtype=F32) / ROPE_DIMS)
    inv2 = jnp.concatenate([inv, inv])
    ang = pos[:, None] * inv2[None]
    tab = _rope_tables(jnp.concatenate([ang, jnp.zeros((max_len, HEAD_DIM - ROPE_DIMS), F32)], -1), ROPE_DIMS)
    keep = jnp.asarray(np.tile(np.arange(HEAD_DIM) < ROPE_DIMS, 2))
    return jnp.stack([jnp.where(keep, tab[0], 1.0), jnp.where(keep, tab[1], 0.0), jnp.where(keep, tab[2], 0.0)])


def _natten_bias(rpb):
    c, kc = np.arange(GRID_W)[:, None], np.arange(GRID_W)[None, :]
    cs = np.clip(c - NA_COLS // 2, 0, GRID_W - NA_COLS)
    col_ok = (kc >= cs) & (kc < cs + NA_COLS)
    col_pick = (kc - c + NA_COLS - 1)[None] == np.arange(2 * NA_COLS - 1)[:, None, None]
    by_col = jnp.einsum("hdm,mcq->hdcq", rpb.astype(F32), jnp.asarray(col_pick, F32),
                        precision=lax.Precision.HIGHEST)
    rl, ki = np.arange(NA_QROWS)[:, None], np.arange(NA_KROWS)[None, :]
    out = []
    for delta, first in ((0, 0 * rl), (NA_QROWS, rl), (2 * NA_QROWS, 0 * rl + NA_QROWS)):
        row_ok = (ki >= first) & (ki < first + NA_ROWS)
        row_pick = ((ki - delta - rl + NA_ROWS - 1)[None] == np.arange(2 * NA_ROWS - 1)[:, None, None]) & row_ok
        b = jnp.einsum("drk,hdcq->hrckq", jnp.asarray(row_pick, F32), by_col, precision=lax.Precision.HIGHEST)
        ok = row_ok[:, None, :, None] & col_ok[None, :, None, :]
        b = jnp.where(jnp.asarray(ok)[None], b * LOG2E, NEG)
        out.append(b.reshape(rpb.shape[0], NA_QROWS * GRID_W, NA_KROWS * GRID_W))
    return jnp.stack(out)


def _inproj_even_body(x_ref, g_ref, w_ref, tab_ref, hg_ref, oa_ref, ob_ref, avt_ref):
    hn = _rms(x_ref[...], g_ref[...]).astype(BF16)
    low = _lane_iota() < HEAD_DIM

    def head_norm_rope(y, gain, scale):
        ss = y * y
        s_lo = jnp.sum(jnp.where(low, ss, 0.0), axis=-1, keepdims=True)
        s_hi = jnp.sum(jnp.where(low, 0.0, ss), axis=-1, keepdims=True)
        y = y * lax.rsqrt(jnp.where(low, s_lo, s_hi) * (1.0 / HEAD_DIM) + EPS) * gain
        return _rope(y, tab_ref, HEAD_DIM // 4) * scale

    for j in range(A_Q_W // LANES):
        c = j * LANES
        oa_ref[:, c:c + LANES] = head_norm_rope(_dot(hn, w_ref[:, c:c + LANES]), hg_ref[0:1, :], SCALE).astype(BF16)
    c = A_Q_W
    oa_ref[:, c:c + LANES] = head_norm_rope(_dot(hn, w_ref[:, c:c + LANES]), hg_ref[1:2, :], 1.0).astype(BF16)
    c = A_Q_W + A_KV_W
    avt_ref[...] = _dot(hn, w_ref[:, c:c + LANES]).T.astype(BF16)
    c0 = A_Q_W + 2 * A_KV_W
    for j in range(3 * B_W // 256):
        c = j * 256
        y = _dot(hn, w_ref[:, c0 + c:c0 + c + 256])
        ob_ref[:, c:c + 256] = (y * SCALE if c < B_W else y).astype(BF16)


def _inproj_odd_body(x_ref, g_ref, w_ref, tab_ref, oc_ref, cvt_ref):
    hn = _rms(x_ref[...], g_ref[...]).astype(BF16)
    for j in range(2 * D_MODEL // LANES):
        c = j * LANES
        y = _rope(_dot(hn, w_ref[:, c:c + LANES]), tab_ref, ROPE_DIMS // 2)
        oc_ref[:, c:c + LANES] = (y * SCALE if c < D_MODEL else y).astype(BF16)
    for j in range(D_MODEL // 256):
        c = j * 256
        cvt_ref[c:c + 256, :] = _dot(hn, w_ref[:, 2 * D_MODEL + c:2 * D_MODEL + c + 256]).T.astype(BF16)


def _tab_index(seqs, tm):
    (r0, _, l0), (r1, _, l1) = seqs

    def index(i):
        return (0, jnp.where(i < r1 // tm, (i - r0 // tm) % (l0 // tm), (i - r1 // tm) % (l1 // tm)), 0)
    return index


def _inproj(x, gain, w, tab, seqs, head_gain=None):
    t = x.shape[0]
    tm = TOK_TILE
    even = head_gain is not None
    in_specs = [pl.BlockSpec((tm, D_MODEL), lambda i: (i, 0)),
                pl.BlockSpec((1, D_MODEL), lambda i: (0, 0)),
                pl.BlockSpec(w.shape, lambda i: (0, 0)),
                pl.BlockSpec((3, tm, LANES), _tab_index(seqs, tm))]
    args = [x, gain.reshape(1, D_MODEL), w, tab]
    if even:
        in_specs.append(pl.BlockSpec((2, LANES), lambda i: (0, 0)))
        args.append(head_gain)
        widths, t_width = (A_Q_W + A_KV_W, 3 * B_W), A_KV_W
        body = _inproj_even_body
    else:
        widths, t_width = (2 * D_MODEL,), D_MODEL
        body = _inproj_odd_body
    return pl.pallas_call(
        body,
        grid=(t // tm,),
        in_specs=in_specs,
        out_specs=[pl.BlockSpec((tm, n), lambda i: (i, 0)) for n in widths]
        + [pl.BlockSpec((t_width, tm), lambda i: (0, i))],
        out_shape=[jax.ShapeDtypeStruct((t, n), BF16) for n in widths]
        + [jax.ShapeDtypeStruct((t_width, t), BF16)],
        compiler_params=_params("parallel"),
        name="inproj_even" if even else "inproj_odd",
    )(*args)


def _outproj_body(*refs, n_pieces, first_tiles):
    o0, o1 = refs[:n_pieces], refs[n_pieces:2 * n_pieces]
    w_ref, x_ref, g_ref, wr_ref, br_ref, x1_ref, hn_ref, lg_ref = refs[2 * n_pieces:]
    in_first = pl.program_id(0) < first_tiles
    x1 = x_ref[...]
    c = 0
    for a_ref, b_ref in zip(o0, o1):
        n = a_ref.shape[1]
        x1 = x1 + _dot(jnp.where(in_first, a_ref[...], b_ref[...]), w_ref[c:c + n, :])
        c += n
    x1_ref[...] = x1
    hn = _rms(x1, g_ref[...])
    hn_ref[...] = hn
    hi = hn.astype(BF16)
    lo = (hn - hi.astype(F32)).astype(BF16)
    both = _dot(hi, wr_ref[...])
    lg_ref[...] = both[:, :LANES] + both[:, LANES:] + _dot(lo, wr_ref[:, :LANES]) + br_ref[...]


def _outproj(pieces, w, x, gain, w_router, b_router):
    t = x.shape[0]
    tm = TOK_TILE
    row = lambda i: (i, 0)
    fixed = lambda i: (0, 0)
    first_tiles = pieces[0][0].shape[0] // tm
    piece_specs = (
        [pl.BlockSpec((tm, p.shape[1]), lambda i: (jnp.minimum(i, first_tiles - 1), 0)) for p in pieces[0]]
        + [pl.BlockSpec((tm, p.shape[1]), lambda i: (jnp.maximum(i - first_tiles, 0), 0)) for p in pieces[1]])
    return pl.pallas_call(
        functools.partial(_outproj_body, n_pieces=len(pieces[0]), first_tiles=first_tiles),
        grid=(t // tm,),
        in_specs=piece_specs + [
            pl.BlockSpec((D_MODEL, D_MODEL), fixed),
            pl.BlockSpec((tm, D_MODEL), row), pl.BlockSpec((1, D_MODEL), fixed),
            pl.BlockSpec((D_MODEL, 2 * LANES), fixed), pl.BlockSpec((1, LANES), fixed)],
        out_specs=[pl.BlockSpec((tm, D_MODEL), row), pl.BlockSpec((tm, D_MODEL), row),
                   pl.BlockSpec((tm, LANES), row)],
        out_shape=[jax.ShapeDtypeStruct((t, D_MODEL), F32), jax.ShapeDtypeStruct((t, D_MODEL), F32),
                   jax.ShapeDtypeStruct((t, LANES), F32)],
        compiler_params=_params("parallel"),
        name="outproj_router",
    )(*pieces[0], *pieces[1], w, x, gain.reshape(1, D_MODEL), w_router, b_router)


def _softmax_pv(s, v):
    m = jnp.max(s, axis=-1, keepdims=True)
    p = jnp.exp2(s - m)
    l = jnp.sum(p, axis=-1, keepdims=True)
    return _dot(p.astype(BF16), v) / l


def _attend_t(chains, n_keys):
    tq = chains[0][1].shape[0]
    state = [(jnp.full((1, tq), NEG, F32), jnp.zeros((1, tq), F32), None) for _ in chains]
    chunk = min(KEY_CHUNK, n_keys)
    units = [(n, slice(t * chunk, (t + 1) * chunk)) for t in range(n_keys // chunk) for n in range(len(chains))]

    def scores(unit):
        n, rows = unit
        k_ref, q, _ = chains[n]
        return _dot_nt(k_ref[rows, :], q)

    s_next = scores(units[0])
    for u, (n, rows) in enumerate(units):
        s = s_next
        if u + 1 < len(units):
            s_next = scores(units[u + 1])
        m, l, acc = state[n]
        m_new = jnp.maximum(m, jnp.max(s, axis=0, keepdims=True))
        alpha = jnp.exp2(m - m_new)
        p = jnp.exp2(s - m_new)
        l = alpha * l + jnp.sum(p, axis=0, keepdims=True)
        pv = _dot(chains[n][2][:, rows], p.astype(BF16))
        state[n] = (m_new, l, pv if acc is None else alpha * acc + pv)
    return [acc / l for _, l, acc in state]


def _gqa_body(q_ref, k_ref, vt_ref, o_ref):
    lane = _lane_iota()
    chains = []
    for j in range(A_Q_W // LANES):
        g = j // 2
        keep = (lane >= g * HEAD_DIM) & (lane < (g + 1) * HEAD_DIM)
        q2 = q_ref[:, j * LANES:(j + 1) * LANES].astype(F32)
        for par in range(2):
            qh = q2 if par == g else pltpu.roll(q2, HEAD_DIM, 1)
            qh = jnp.where(keep, qh, 0.0).astype(BF16)
            chains.append((k_ref, qh, vt_ref.at[g * HEAD_DIM:(g + 1) * HEAD_DIM, :]))
    outs = _attend_t(chains, k_ref.shape[0])
    for j in range(A_Q_W // LANES):
        o_ref[:, j * LANES:(j + 1) * LANES] = jnp.concatenate(outs[2 * j:2 * j + 2], axis=0).T.astype(BF16)


def _natten_body(q_ref, k_ref, v_ref, bias_ref, o_ref, *, rows):
    rb = pl.program_id(1)
    start = pl.multiple_of(jnp.clip(rb * NA_QROWS - NA_ROWS // 2, 0, rows - NA_KROWS) * GRID_W, GRID_W)
    lane = _lane_iota()
    nk = NA_KROWS * GRID_W
    for j in range(B_W // LANES):
        kk = k_ref[pl.ds(start, nk), j * LANES:(j + 1) * LANES]
        vv = v_ref[pl.ds(start, nk), j * LANES:(j + 1) * LANES]
        q2 = q_ref[:, j * LANES:(j + 1) * LANES].astype(F32)
        res = []
        for par in range(2):
            keep = (lane < HEAD_DIM) if par == 0 else (lane >= HEAD_DIM)
            qh = jnp.where(keep, q2, 0.0).astype(BF16)
            res.append(_softmax_pv(_dot_nt(qh, kk) + bias_ref[0, 2 * j + par], vv))
        o_ref[:, j * LANES:(j + 1) * LANES] = jnp.where(lane < HEAD_DIM, res[0], res[1]).astype(BF16)


def _diff_body(lam_ref, sg_ref, q_ref, k_ref, vt_ref, o_ref, *, lam_init):
    lp = lam_ref[...]
    lam = (jnp.exp(jnp.sum(lp[0:1] * lp[1:2], axis=-1, keepdims=True))
           - jnp.exp(jnp.sum(lp[2:3] * lp[3:4], axis=-1, keepdims=True)) + lam_init)
    lane = _lane_iota()
    chains = []
    for h in range(DIFF_HEADS):
        c = h * LANES
        q = q_ref[:, c:c + LANES].astype(F32)
        for qm in (jnp.where(lane < HEAD_DIM, q, 0.0), jnp.where(lane < HEAD_DIM, 0.0, q)):
            chains.append((k_ref.at[:, c:c + LANES], qm.astype(BF16), vt_ref.at[c:c + LANES, :]))
    outs = _attend_t(chains, k_ref.shape[0])
    for h in range(DIFF_HEADS):
        o = outs[2 * h] - lam * outs[2 * h + 1]
        o = o * lax.rsqrt(jnp.mean(o * o, axis=0, keepdims=True) + EPS) * (1.0 - lam_init)
        o_ref[:, h * LANES:(h + 1) * LANES] = (o.T * sg_ref[...]).astype(BF16)


def _gqa(a, avt, seq):
    r0, nb, L = seq
    tq = ATT_Q
    nq = L // tq
    return pl.pallas_call(
        _gqa_body, grid=(nb, nq),
        in_specs=[pl.BlockSpec((tq, A_Q_W), lambda b, i: (r0 // tq + b * nq + i, 0)),
                  pl.BlockSpec((L, LANES), lambda b, i: (r0 // L + b, A_Q_W // LANES)),
                  pl.BlockSpec((LANES, L), lambda b, i: (0, r0 // L + b))],
        out_specs=pl.BlockSpec((tq, A_Q_W), lambda b, i: (b * nq + i, 0)),
        out_shape=jax.ShapeDtypeStruct((nb * L, A_Q_W), BF16),
        compiler_params=_params("parallel", "parallel"), name="gqa_axial")(a, a, avt)


def _natten(bq, bias, seq):
    r0, nb, L = seq
    rows = L // GRID_W
    assert rows >= NA_KROWS and rows % NA_QROWS == 0
    tq = NA_QROWS * GRID_W
    nq = L // tq

    def bias_index(b, i):
        return (jnp.where(i == 0, 0, jnp.where(i == nq - 1, 2, 1)), 0, 0, 0)

    return pl.pallas_call(
        functools.partial(_natten_body, rows=rows), grid=(nb, nq),
        in_specs=[pl.BlockSpec((tq, B_W), lambda b, i: (r0 // tq + b * nq + i, 0)),
                  pl.BlockSpec((L, B_W), lambda b, i: (r0 // L + b, 1)),
                  pl.BlockSpec((L, B_W), lambda b, i: (r0 // L + b, 2)),
                  pl.BlockSpec((1,) + bias.shape[1:], bias_index)],
        out_specs=pl.BlockSpec((tq, B_W), lambda b, i: (b * nq + i, 0)),
        out_shape=jax.ShapeDtypeStruct((nb * L, B_W), BF16),
        compiler_params=_params("parallel", "arbitrary"), name="natten")(bq, bq, bq, bias)


def _diff(c, cvt, lam_rows, sub_gain, lam_init, seq):
    r0, nb, L = seq
    tq = ATT_Q
    nq = L // tq
    w = DIFF_HEADS * LANES
    ng = D_MODEL // w
    fixed = lambda b, h, i: (0, 0)
    return pl.pallas_call(
        functools.partial(_diff_body, lam_init=lam_init), grid=(nb, ng, nq),
        in_specs=[pl.BlockSpec((8, LANES), fixed), pl.BlockSpec((1, LANES), fixed),
                  pl.BlockSpec((tq, w), lambda b, h, i: (r0 // tq + b * nq + i, h)),
                  pl.BlockSpec((L, w), lambda b, h, i: (r0 // L + b, ng + h)),
                  pl.BlockSpec((w, L), lambda b, h, i: (h, r0 // L + b))],
        out_specs=pl.BlockSpec((tq, w), lambda b, h, i: (b * nq + i, h)),
        out_shape=jax.ShapeDtypeStruct((nb * L, D_MODEL), BF16),
        compiler_params=_params("parallel", "parallel", "parallel"), name="diff_attn",
    )(lam_rows, sub_gain, c, c, cvt)


def _route_body(lg_ref, meta_ref, cnt_ref, run_ref):
    @pl.when(pl.program_id(0) == 0)
    def _():
        run_ref[...] = jnp.zeros_like(run_ref)

    lg = lg_ref[...]
    tm = lg.shape[0]
    lane_i = _lane_iota(lg.shape)
    lane = lane_i.astype(F32)
    far = float(LANES)
    is_g = lane_i < N_GROUPS
    gl = jnp.where(is_g, lg, NEG)
    gmax = jnp.max(gl, axis=-1, keepdims=True)
    gidx = jnp.min(jnp.where(is_g & (gl == gmax), lane, far), axis=-1, keepdims=True)
    g_w = 1.0 / jnp.sum(jnp.where(is_g, jnp.exp(gl - gmax), 0.0), axis=-1, keepdims=True)
    eid_i = lane_i - N_GROUPS
    eid = eid_i.astype(F32)
    grp = lax.shift_right_arithmetic(eid_i, int(math.log2(EXPERTS_PER_GROUP))).astype(F32)
    in_grp = (eid_i >= 0) & (eid_i < N_EXPERTS) & (grp == gidx)
    el = jnp.where(in_grp, lg, NEG)
    e1 = jnp.max(el, axis=-1, keepdims=True)
    i1 = jnp.min(jnp.where(in_grp & (el == e1), eid, far), axis=-1, keepdims=True)
    rest = in_grp & (eid != i1)
    el2 = jnp.where(rest, lg, NEG)
    e2 = jnp.max(el2, axis=-1, keepdims=True)
    i2 = jnp.min(jnp.where(rest & (el2 == e2), eid, far), axis=-1, keepdims=True)
    t = jnp.exp(e2 - e1)
    w1 = g_w / (1.0 + t)
    w2 = g_w * t / (1.0 + t)
    pick = ((lane == i1) | (lane == i2))
    onehot = jnp.where(pick, 1.0, 0.0)
    r = lax.broadcasted_iota(I32, (tm, tm), 0)
    c = lax.broadcasted_iota(I32, (tm, tm), 1)
    before = jnp.where(c < r, 1.0, 0.0).astype(BF16)
    prefix = _dot(before, onehot.astype(BF16)) + run_ref[...]
    rank1 = jnp.sum(jnp.where(lane == i1, prefix, 0.0), axis=-1, keepdims=True)
    rank2 = jnp.sum(jnp.where(lane == i2, prefix, 0.0), axis=-1, keepdims=True)
    run = run_ref[...] + jnp.sum(onehot, axis=0, keepdims=True)
    run_ref[...] = run
    cnt_ref[...] = run
    meta = jnp.zeros(lg.shape, F32)
    for n, col in enumerate((i1, i2, w1, w2, rank1, rank2)):
        meta = jnp.where(lane_i == n, col, meta)
    meta_ref[...] = meta


def _route(logits):
    t = logits.shape[0]
    tm = ROUTE_TILE
    return pl.pallas_call(
        _route_body,
        grid=(t // tm,),
        in_specs=[pl.BlockSpec((tm, LANES), lambda i: (i, 0))],
        out_specs=[pl.BlockSpec((tm, LANES), lambda i: (i, 0)), pl.BlockSpec((1, LANES), lambda i: (0, 0))],
        out_shape=[jax.ShapeDtypeStruct((t, LANES), F32), jax.ShapeDtypeStruct((1, LANES), F32)],
        scratch_shapes=[pltpu.VMEM((1, LANES), F32)],
        compiler_params=_params("arbitrary"),
        name="route",
    )(logits)


def _row_copy(src, i, dst, j, sem):
    return pltpu.make_async_copy(src.at[pl.ds(i, 1), :], dst.at[pl.ds(j, 1), :], sem)


def _dispatch_body(dest_ref, hn_ref, xs_in_ref, xs_ref, sem):
    del xs_in_ref
    tm = hn_ref.shape[0]

    def issue(k0, carry):
        for u in range(MOVE_UNROLL):
            k = k0 * MOVE_UNROLL + u
            _row_copy(hn_ref, k, xs_ref, dest_ref[0, 0, k], sem).start()
            _row_copy(hn_ref, k, xs_ref, dest_ref[0, 0, tm + k], sem).start()
        return carry

    lax.fori_loop(0, tm // MOVE_UNROLL, issue, 0)
    for _ in range(2):
        pltpu.make_async_copy(hn_ref, xs_ref.at[pl.ds(0, tm), :], sem).wait()


def _dispatch(hn, dest, n_rows):
    t = hn.shape[0]
    tm = MOVE_TILE
    zeros = jnp.zeros((n_rows, D_MODEL), F32)
    return pl.pallas_call(
        _dispatch_body,
        grid=(t // tm,),
        in_specs=[pl.BlockSpec((1, 1, 2 * tm), lambda i: (i, 0, 0), memory_space=pltpu.SMEM),
                  pl.BlockSpec((tm, D_MODEL), lambda i: (i, 0)),
                  pl.BlockSpec(memory_space=pl.ANY)],
        out_specs=pl.BlockSpec(memory_space=pl.ANY),
        out_shape=jax.ShapeDtypeStruct((n_rows, D_MODEL), F32),
        scratch_shapes=[pltpu.SemaphoreType.DMA],
        input_output_aliases={2: 0},
        compiler_params=_params("arbitrary"),
        name="moe_dispatch",
    )(dest, hn, zeros)


def _expert_body(te_ref, used_ref, x_ref, wg_ref, wu_ref, wd_ref, y_ref):
    i = pl.program_id(0)

    @pl.when(i < used_ref[0])
    def _():
        xb = x_ref[...].astype(BF16)
        g = _dot(xb, wg_ref[0])
        u = _dot(xb, wu_ref[0])
        h = (g / (1.0 + jnp.exp(-g)) * u).astype(BF16)
        y_ref[...] = _dot(h, wd_ref[0])

    @pl.when(i >= used_ref[0])
    def _():
        y_ref[...] = jnp.zeros_like(y_ref)


def _experts(xs, tile_expert, n_used, w_gate, w_up, w_down):
    n_rows = xs.shape[0]
    tm = EXPERT_TILE
    grid_spec = pltpu.PrefetchScalarGridSpec(
        num_scalar_prefetch=2,
        grid=(n_rows // tm,),
        in_specs=[pl.BlockSpec((tm, D_MODEL), lambda i, te, nu: (i, 0)),
                  pl.BlockSpec((1, D_MODEL, D_EXPERT), lambda i, te, nu: (te[i], 0, 0)),
                  pl.BlockSpec((1, D_MODEL, D_EXPERT), lambda i, te, nu: (te[i], 0, 0)),
                  pl.BlockSpec((1, D_EXPERT, D_MODEL), lambda i, te, nu: (te[i], 0, 0))],
        out_specs=pl.BlockSpec((tm, D_MODEL), lambda i, te, nu: (i, 0)),
    )
    return pl.pallas_call(
        _expert_body,
        grid_spec=grid_spec,
        out_shape=jax.ShapeDtypeStruct((n_rows, D_MODEL), F32),
        compiler_params=_params("arbitrary"),
        name="moe_experts",
    )(tile_expert, n_used, xs, w_gate, w_up, w_down)


def _combine_body(dest_ref, x_ref, meta_ref, ys_ref, *rest, final):
    if final:
        g_ref, o_ref, ya_ref, yb_ref, sem = rest
    else:
        o_ref, ya_ref, yb_ref, sem = rest
    tm = x_ref.shape[0]

    def issue(k0, carry):
        for u in range(MOVE_UNROLL):
            k = k0 * MOVE_UNROLL + u
            _row_copy(ys_ref, dest_ref[0, 0, k], ya_ref, k, sem).start()
            _row_copy(ys_ref, dest_ref[0, 0, tm + k], yb_ref, k, sem).start()
        return carry

    lax.fori_loop(0, tm // MOVE_UNROLL, issue, 0)
    for buf in (ya_ref, yb_ref):
        pltpu.make_async_copy(ys_ref.at[pl.ds(0, tm), :], buf, sem).wait()
    meta = meta_ref[...]
    x = x_ref[...] + meta[:, 2:3] * ya_ref[...] + meta[:, 3:4] * yb_ref[...]
    o_ref[...] = _rms(x, g_ref[...]) if final else x


def _combine(x1, meta, dest, ys, r0, rows, final_gain=None):
    tm = MOVE_TILE
    first = r0 // tm
    final = final_gain is not None
    tile = lambda i: (first + i, 0)
    in_specs = [pl.BlockSpec((1, 1, 2 * tm), lambda i: (first + i, 0, 0), memory_space=pltpu.SMEM),
                pl.BlockSpec((tm, D_MODEL), tile),
                pl.BlockSpec((tm, LANES), tile),
                pl.BlockSpec(memory_space=pl.ANY)]
    args = [dest, x1, meta, ys]
    if final:
        in_specs.append(pl.BlockSpec((1, D_MODEL), lambda i: (0, 0)))
        args.append(final_gain.reshape(1, D_MODEL))
    return pl.pallas_call(
        functools.partial(_combine_body, final=final),
        grid=(rows // tm,),
        in_specs=in_specs,
        out_specs=pl.BlockSpec((tm, D_MODEL), lambda i: (i, 0)),
        out_shape=jax.ShapeDtypeStruct((rows, D_MODEL), F32),
        scratch_shapes=[pltpu.VMEM((tm, D_MODEL), F32), pltpu.VMEM((tm, D_MODEL), F32),
                        pltpu.SemaphoreType.DMA],
        compiler_params=_params("arbitrary"),
        name="moe_combine_final" if final else "moe_combine",
    )(*args)


def _moe(x1, hn, logits, w_gate, w_up, w_down, final=None):
    t = x1.shape[0]
    tm = MOVE_TILE
    meta, counts = _route(logits)
    counts = counts[0, :N_EXPERTS].astype(I32)
    padded = (counts + EXPERT_TILE - 1) // EXPERT_TILE * EXPERT_TILE
    ends = jnp.cumsum(padded)
    offsets = ends - padded
    n_rows = 2 * t + N_EXPERTS * EXPERT_TILE
    n_tiles = n_rows // EXPERT_TILE
    n_used = (ends[-1] // EXPERT_TILE).astype(I32)
    tile_start = jnp.minimum(jnp.arange(n_tiles, dtype=I32), n_used - 1) * EXPERT_TILE
    tile_expert = jnp.sum((ends[None, :] <= tile_start[:, None]).astype(I32), axis=1)
    experts = jnp.arange(N_EXPERTS, dtype=I32)[None, :]

    def position(e_col, rank_col):
        e = meta[:, e_col].astype(I32)[:, None]
        return jnp.sum(jnp.where(e == experts, offsets[None, :], 0), axis=1) + meta[:, rank_col].astype(I32)

    dest = jnp.concatenate([position(0, 4).reshape(t // tm, 1, tm), position(1, 5).reshape(t // tm, 1, tm)], axis=-1)
    xs = _dispatch(hn, dest, n_rows)
    ys = _experts(xs, tile_expert, n_used.reshape(1), w_gate, w_up, w_down)
    if final is None:
        return _combine(x1, meta, dest, ys, 0, t)
    gain, ranges = final
    return [_combine(x1, meta, dest, ys, r0, rows, gain) for r0, rows in ranges]


def kernel(x_prompt, x_sample, norm_mix, norm_ffn, norm_final, w_in_even, q_gain, k_gain, rpb, w_out_even,
           w_in_odd, lam_q1, lam_k1, lam_q2, lam_k2, subln_gain, w_out_odd,
           w_rg, b_rg, w_re, b_re, w_gate, w_up, w_down):
    bp, lp, d = x_prompt.shape
    bs, ls, _ = x_sample.shape
    assert d == D_MODEL and (bp * lp) % ls == 0 and lp % MOVE_TILE == 0 and ls % MOVE_TILE == 0
    tp, ts = bp * lp, bs * ls
    t = tp + ts
    seqs = ((0, bp, lp), (tp, bs, ls))
    x = jnp.concatenate([x_prompt.reshape(tp, d), x_sample.reshape(ts, d)], axis=0)
    max_len = max(lp, ls)
    depth = norm_mix.shape[0]

    def router(i):
        w = jnp.zeros((D_MODEL, LANES), F32)
        w = w.at[:, :N_GROUPS].set(w_rg[i]).at[:, N_GROUPS:N_GROUPS + N_EXPERTS].set(w_re[i])
        b = jnp.zeros((1, LANES), F32)
        b = b.at[0, :N_GROUPS].set(b_rg[i]).at[0, N_GROUPS:N_GROUPS + N_EXPERTS].set(b_re[i])
        w_hi = w.astype(BF16)
        w_lo = (w - w_hi.astype(F32)).astype(BF16)
        return jnp.concatenate([w_hi, w_lo], axis=1), b

    for i in range(depth):
        j = i // 2
        if i % 2 == 0:
            head_gain = jnp.stack([jnp.tile(q_gain[j], 2), jnp.tile(k_gain[j], 2)])
            a, b, avt = _inproj(x, norm_mix[i], w_in_even[j].astype(BF16), _axial_tables(max_len), seqs, head_gain)
            bias = _natten_bias(rpb[j])
            pieces = [[_gqa(a, avt, seq), _natten(b, bias, seq)] for seq in seqs]
            w_out = w_out_even[j]
        else:
            c, cvt = _inproj(x, norm_mix[i], w_in_odd[j].astype(BF16), _partial_tables(max_len), seqs)
            lam_init = 0.8 - 0.6 * math.exp(-0.3 * i)
            lam_rows = jnp.zeros((8, LANES), F32).at[:4, :HEAD_DIM].set(
                jnp.stack([lam_q1[j], lam_k1[j], lam_q2[j], lam_k2[j]]))
            pieces = [[_diff(c, cvt, lam_rows, subln_gain[j].reshape(1, LANES), lam_init, seq)] for seq in seqs]
            w_out = w_out_odd[j]
        w_router, b_router = router(i)
        x1, hn, logits = _outproj(pieces, w_out.astype(BF16), x, norm_ffn[i], w_router, b_router)
        final = (norm_final, ((0, tp), (tp, ts))) if i == depth - 1 else None
        x = _moe(x1, hn, logits, w_gate[i].astype(BF16), w_up[i].astype(BF16), w_down[i].astype(BF16), final)
    y_prompt, y_sample = x
    return (y_prompt.reshape(bp, lp, d), y_sample.reshape(bs, ls, d))
```

```python
import functools
import math

import numpy as np
import jax
import jax.numpy as jnp
from jax import lax
from jax.experimental import pallas as pl
from jax.experimental.pallas import tpu as pltpu

F32 = jnp.float32
BF16 = jnp.bfloat16
I32 = jnp.int32

D_MODEL = 1024
HEAD_DIM = 64
GRID_W = 64
EPS = 1e-6
LANES = 128
MXU_W = 256
A_Q_W, A_KV_W, B_W = 512, 128, 512
EVEN_IN = A_Q_W + 2 * A_KV_W + 3 * B_W
ODD_IN = 3 * D_MODEL
AXIAL_THETA = 10000.0
ROPE_THETA = 500000.0
ROPE_DIMS = HEAD_DIM // 4
NA_ROWS, NA_COLS = 8, 16
NA_QROWS = 4
NA_KROWS = NA_QROWS + NA_ROWS
N_GROUPS, EXPERTS_PER_GROUP = 4, 8
N_EXPERTS = N_GROUPS * EXPERTS_PER_GROUP
D_EXPERT = 512
LOG2E = math.log2(math.e)
SCALE = HEAD_DIM ** -0.5 * LOG2E
NEG = -1e30

VMEM_LIMIT_BYTES = 56 * 1024 * 1024
TOK_TILE = 512
ROUTE_TILE = 512
MOVE_TILE = 1024
MOVE_UNROLL = 8
EXPERT_TILE = 512
ATT_Q = 256
DIFF_HEADS = 4
KEY_CHUNK = 4096
SCORES_AHEAD = 4


def _params(*sem):
    return pltpu.CompilerParams(dimension_semantics=sem, vmem_limit_bytes=VMEM_LIMIT_BYTES)


def _lane_iota(shape=(1, LANES)):
    return lax.broadcasted_iota(I32, shape, len(shape) - 1)


def _rms(x, gain):
    return x * lax.rsqrt(jnp.mean(x * x, axis=-1, keepdims=True) + EPS) * gain


def _dot(a, b):
    return jnp.dot(a, b, preferred_element_type=F32)


def _dot_nt(a, b):
    return lax.dot_general(a, b, (((1,), (1,)), ((), ())), preferred_element_type=F32)


def _rope(y, tab_ref, shift):
    return (y * tab_ref[0] + pltpu.roll(y, LANES - shift, 1) * tab_ref[1]
            + pltpu.roll(y, shift, 1) * tab_ref[2])


def _rope_tables(angles, n):
    cos = jnp.cos(angles)
    sin = jnp.sin(angles)
    low = jnp.asarray((np.arange(HEAD_DIM) % n) < n // 2)
    tab = jnp.stack([cos, jnp.where(low, -sin, 0.0), jnp.where(low, 0.0, sin)])
    return jnp.concatenate([tab, tab], axis=-1).astype(F32)


def _axial_tables(max_len):
    pos = jnp.arange(max_len)
    row = (pos // GRID_W).astype(F32)
    col = (pos % GRID_W).astype(F32)
    half = HEAD_DIM // 2
    inv = AXIAL_THETA ** (-jnp.arange(0, half, 2, dtype=F32) / half)
    inv2 = jnp.concatenate([inv, inv])
    ang = jnp.concatenate([row[:, None] * inv2[None], col[:, None] * inv2[None]], axis=-1)
    return _rope_tables(ang, half)


def _partial_tables(max_len):
    pos = jnp.arange(max_len).astype(F32)
    inv = ROPE_THETA ** (-jnp.arange(0, ROPE_DIMS, 2, dtype=F32) / ROPE_DIMS)
    inv2 = jnp.concatenate([inv, inv])
    ang = pos[:, None] * inv2[None]
    tab = _rope_tables(jnp.concatenate([ang, jnp.zeros((max_len, HEAD_DIM - ROPE_DIMS), F32)], -1), ROPE_DIMS)
    keep = jnp.asarray(np.tile(np.arange(HEAD_DIM) < ROPE_DIMS, 2))
    return jnp.stack([jnp.where(keep, tab[0], 1.0), jnp.where(keep, tab[1], 0.0), jnp.where(keep, tab[2], 0.0)])


def _natten_bias(rpb):
    c, kc = np.arange(GRID_W)[:, None], np.arange(GRID_W)[None, :]
    cs = np.clip(c - NA_COLS // 2, 0, GRID_W - NA_COLS)
    col_ok = (kc >= cs) & (kc < cs + NA_COLS)
    col_pick = (kc - c + NA_COLS - 1)[None] == np.arange(2 * NA_COLS - 1)[:, None, None]
    by_col = jnp.einsum("hdm,mcq->hdcq", rpb.astype(F32), jnp.asarray(col_pick, F32),
                        precision=lax.Precision.HIGHEST)
    rl, ki = np.arange(NA_QROWS)[:, None], np.arange(NA_KROWS)[None, :]
    out = []
    for delta, first in ((0, 0 * rl), (NA_QROWS, rl), (2 * NA_QROWS, 0 * rl + NA_QROWS)):
        row_ok = (ki >= first) & (ki < first + NA_ROWS)
        row_pick = ((ki - delta - rl + NA_ROWS - 1)[None] == np.arange(2 * NA_ROWS - 1)[:, None, None]) & row_ok
        b = jnp.einsum("drk,hdcq->hrckq", jnp.asarray(row_pick, F32), by_col, precision=lax.Precision.HIGHEST)
        ok = row_ok[:, None, :, None] & col_ok[None, :, None, :]
        b = jnp.where(jnp.asarray(ok)[None], b * LOG2E, NEG)
        out.append(b.reshape(rpb.shape[0], NA_QROWS * GRID_W, NA_KROWS * GRID_W))
    return jnp.stack(out)


def _rows_specs(parts, tm):
    if len(parts) == 1:
        return [pl.BlockSpec((tm, D_MODEL), lambda i: (i, 0))], 0
    first = parts[0].shape[0] // tm
    return [pl.BlockSpec((tm, D_MODEL), lambda i: (jnp.minimum(i, first - 1), 0)),
            pl.BlockSpec((tm, D_MODEL), lambda i: (jnp.maximum(i - first, 0), 0))], first


def _rows_value(refs, first_tiles):
    if len(refs) == 1:
        return refs[0][...]
    return jnp.where(pl.program_id(0) < first_tiles, refs[0][...], refs[1][...])


def _inproj_even_body(*refs, n_x, first_tiles):
    g_ref, w_ref, tab_ref, hg_ref, oa_ref, ob_ref, avt_ref = refs[n_x:]
    hn = _rms(_rows_value(refs[:n_x], first_tiles), g_ref[...]).astype(BF16)
    low = _lane_iota() < HEAD_DIM

    def head_norm_rope(y, gain, scale):
        ss = y * y
        s_lo = jnp.sum(jnp.where(low, ss, 0.0), axis=-1, keepdims=True)
        s_hi = jnp.sum(jnp.where(low, 0.0, ss), axis=-1, keepdims=True)
        y = y * lax.rsqrt(jnp.where(low, s_lo, s_hi) * (1.0 / HEAD_DIM) + EPS) * gain
        return _rope(y, tab_ref, HEAD_DIM // 4) * scale

    for c in range(0, A_Q_W, MXU_W):
        y = _dot(hn, w_ref[:, c:c + MXU_W])
        for h in range(0, MXU_W, LANES):
            oa_ref[:, c + h:c + h + LANES] = head_norm_rope(y[:, h:h + LANES], hg_ref[0:1, :], SCALE).astype(BF16)
    y = _dot(hn, w_ref[:, A_Q_W:A_Q_W + MXU_W])
    oa_ref[:, A_Q_W:A_Q_W + LANES] = head_norm_rope(y[:, :LANES], hg_ref[1:2, :], 1.0).astype(BF16)
    avt_ref[...] = y[:, LANES:].T.astype(BF16)
    c0 = A_Q_W + 2 * A_KV_W
    for c in range(0, 3 * B_W, MXU_W):
        y = _dot(hn, w_ref[:, c0 + c:c0 + c + MXU_W])
        ob_ref[:, c:c + MXU_W] = (y * SCALE if c < B_W else y).astype(BF16)


def _inproj_odd_body(*refs, n_x, first_tiles):
    g_ref, w_ref, tab_ref, oc_ref, cvt_ref = refs[n_x:]
    hn = _rms(_rows_value(refs[:n_x], first_tiles), g_ref[...]).astype(BF16)
    for c in range(0, 2 * D_MODEL, MXU_W):
        y = _dot(hn, w_ref[:, c:c + MXU_W])
        for h in range(0, MXU_W, LANES):
            z = _rope(y[:, h:h + LANES], tab_ref, ROPE_DIMS // 2)
            oc_ref[:, c + h:c + h + LANES] = (z * SCALE if c < D_MODEL else z).astype(BF16)
    for c in range(0, D_MODEL, MXU_W):
        cvt_ref[c:c + MXU_W, :] = _dot(hn, w_ref[:, 2 * D_MODEL + c:2 * D_MODEL + c + MXU_W]).T.astype(BF16)


def _tab_index(seqs, tm):
    (r0, _, l0), (r1, _, l1) = seqs

    def index(i):
        return (0, jnp.where(i < r1 // tm, (i - r0 // tm) % (l0 // tm), (i - r1 // tm) % (l1 // tm)), 0)
    return index


def _inproj(x_parts, gain, w, tab, seqs, head_gain=None):
    t = sum(p.shape[0] for p in x_parts)
    tm = TOK_TILE
    even = head_gain is not None
    x_specs, first_tiles = _rows_specs(x_parts, tm)
    in_specs = x_specs + [pl.BlockSpec((1, D_MODEL), lambda i: (0, 0)),
                          pl.BlockSpec(w.shape, lambda i: (0, 0)),
                          pl.BlockSpec((3, tm, LANES), _tab_index(seqs, tm))]
    args = list(x_parts) + [gain.reshape(1, D_MODEL), w, tab]
    if even:
        in_specs.append(pl.BlockSpec((2, LANES), lambda i: (0, 0)))
        args.append(head_gain)
        widths, t_width = (A_Q_W + A_KV_W, 3 * B_W), A_KV_W
        body = _inproj_even_body
    else:
        widths, t_width = (2 * D_MODEL,), D_MODEL
        body = _inproj_odd_body
    return pl.pallas_call(
        functools.partial(body, n_x=len(x_parts), first_tiles=first_tiles),
        grid=(t // tm,),
        in_specs=in_specs,
        out_specs=[pl.BlockSpec((tm, n), lambda i: (i, 0)) for n in widths]
        + [pl.BlockSpec((t_width, tm), lambda i: (0, i))],
        out_shape=[jax.ShapeDtypeStruct((t, n), BF16) for n in widths]
        + [jax.ShapeDtypeStruct((t_width, t), BF16)],
        compiler_params=_params("parallel"),
        name="inproj_even" if even else "inproj_odd",
    )(*args)


def _outproj_body(*refs, n_pieces, n_x, first_tiles):
    o0, o1 = refs[:n_pieces], refs[n_pieces:2 * n_pieces]
    x_refs = refs[2 * n_pieces:2 * n_pieces + n_x]
    w_ref, g_ref, wr_ref, br_ref, x1_ref, hn_ref, lg_ref = refs[2 * n_pieces + n_x:]
    in_first = pl.program_id(0) < first_tiles
    x1 = _rows_value(x_refs, first_tiles)
    c = 0
    for a_ref, b_ref in zip(o0, o1):
        n = a_ref.shape[1]
        x1 = x1 + _dot(jnp.where(in_first, a_ref[...], b_ref[...]), w_ref[c:c + n, :])
        c += n
    x1_ref[...] = x1
    hn = _rms(x1, g_ref[...])
    hn_ref[...] = hn
    hi = hn.astype(BF16)
    lo = (hn - hi.astype(F32)).astype(BF16)
    both = _dot(hi, wr_ref[...])
    lg_ref[...] = both[:, :LANES] + both[:, LANES:] + _dot(lo, wr_ref[:, :LANES]) + br_ref[...]


def _outproj(pieces, w, x_parts, gain, w_router, b_router):
    t = sum(p.shape[0] for p in x_parts)
    tm = TOK_TILE
    row = lambda i: (i, 0)
    fixed = lambda i: (0, 0)
    first_tiles = pieces[0][0].shape[0] // tm
    piece_specs = (
        [pl.BlockSpec((tm, p.shape[1]), lambda i: (jnp.minimum(i, first_tiles - 1), 0)) for p in pieces[0]]
        + [pl.BlockSpec((tm, p.shape[1]), lambda i: (jnp.maximum(i - first_tiles, 0), 0)) for p in pieces[1]])
    x_specs, x_first = _rows_specs(x_parts, tm)
    assert len(x_parts) == 1 or x_first == first_tiles
    return pl.pallas_call(
        functools.partial(_outproj_body, n_pieces=len(pieces[0]), n_x=len(x_parts), first_tiles=first_tiles),
        grid=(t // tm,),
        in_specs=piece_specs + x_specs + [
            pl.BlockSpec((D_MODEL, D_MODEL), fixed), pl.BlockSpec((1, D_MODEL), fixed),
            pl.BlockSpec((D_MODEL, 2 * LANES), fixed), pl.BlockSpec((1, LANES), fixed)],
        out_specs=[pl.BlockSpec((tm, D_MODEL), row), pl.BlockSpec((tm, D_MODEL), row),
                   pl.BlockSpec((tm, LANES), row)],
        out_shape=[jax.ShapeDtypeStruct((t, D_MODEL), F32), jax.ShapeDtypeStruct((t, D_MODEL), F32),
                   jax.ShapeDtypeStruct((t, LANES), F32)],
        compiler_params=_params("parallel"),
        name="outproj_router",
    )(*pieces[0], *pieces[1], *x_parts, w, gain.reshape(1, D_MODEL), w_router, b_router)


def _softmax_pv(s, v):
    m = jnp.max(s, axis=-1, keepdims=True)
    p = jnp.exp2(s - m)
    l = jnp.sum(p, axis=-1, keepdims=True)
    return _dot(p.astype(BF16), v) / l


def _attend_t(chains, n_keys):
    tq = chains[0][1].shape[0]
    state = [(jnp.full((1, tq), NEG, F32), jnp.zeros((1, tq), F32), None) for _ in chains]
    chunk = min(KEY_CHUNK, n_keys)
    units = [(n, slice(t * chunk, (t + 1) * chunk)) for t in range(n_keys // chunk) for n in range(len(chains))]

    def scores(unit):
        n, rows = unit
        k_ref, q, _ = chains[n]
        return _dot_nt(k_ref[rows, :], q)

    queue = [scores(unit) for unit in units[:SCORES_AHEAD]]
    for u, (n, rows) in enumerate(units):
        s = queue.pop(0)
        if u + SCORES_AHEAD < len(units):
            queue.append(scores(units[u + SCORES_AHEAD]))
        m, l, acc = state[n]
        m_new = jnp.maximum(m, jnp.max(s, axis=0, keepdims=True))
        alpha = jnp.exp2(m - m_new)
        p = jnp.exp2(s - m_new)
        l = alpha * l + jnp.sum(p, axis=0, keepdims=True)
        pv = _dot(chains[n][2][:, rows], p.astype(BF16))
        state[n] = (m_new, l, pv if acc is None else alpha * acc + pv)
    return [acc / l for _, l, acc in state]


def _gqa_body(q_ref, k_ref, vt_ref, o_ref):
    lane = _lane_iota()
    chains = []
    for j in range(A_Q_W // LANES):
        g = j // 2
        keep = (lane >= g * HEAD_DIM) & (lane < (g + 1) * HEAD_DIM)
        q2 = q_ref[:, j * LANES:(j + 1) * LANES].astype(F32)
        for par in range(2):
            qh = q2 if par == g else pltpu.roll(q2, HEAD_DIM, 1)
            qh = jnp.where(keep, qh, 0.0).astype(BF16)
            chains.append((k_ref, qh, vt_ref.at[g * HEAD_DIM:(g + 1) * HEAD_DIM, :]))
    outs = _attend_t(chains, k_ref.shape[0])
    for j in range(A_Q_W // LANES):
        o_ref[:, j * LANES:(j + 1) * LANES] = jnp.concatenate(outs[2 * j:2 * j + 2], axis=0).T.astype(BF16)


def _natten_body(q_ref, k_ref, v_ref, bias_ref, o_ref, *, rows):
    rb = pl.program_id(1)
    start = pl.multiple_of(jnp.clip(rb * NA_QROWS - NA_ROWS // 2, 0, rows - NA_KROWS) * GRID_W, GRID_W)
    lane = _lane_iota()
    nk = NA_KROWS * GRID_W
    heads = []
    for j in range(B_W // LANES):
        q2 = q_ref[:, j * LANES:(j + 1) * LANES].astype(F32)
        for keep in (lane < HEAD_DIM, lane >= HEAD_DIM):
            heads.append((jnp.where(keep, q2, 0.0).astype(BF16), j))

    def scores(h):
        qh, j = heads[h]
        return _dot_nt(qh, k_ref[pl.ds(start, nk), j * LANES:(j + 1) * LANES]) + bias_ref[0, h]

    res = []
    s_next = scores(0)
    for h, (_, j) in enumerate(heads):
        s = s_next
        if h + 1 < len(heads):
            s_next = scores(h + 1)
        res.append(_softmax_pv(s, v_ref[pl.ds(start, nk), j * LANES:(j + 1) * LANES]))
    for j in range(B_W // LANES):
        o_ref[:, j * LANES:(j + 1) * LANES] = jnp.where(lane < HEAD_DIM, res[2 * j], res[2 * j + 1]).astype(BF16)


def _diff_body(lam_ref, sg_ref, q_ref, k_ref, vt_ref, o_ref, *, lam_init):
    lp = lam_ref[...]
    lam = (jnp.exp(jnp.sum(lp[0:1] * lp[1:2], axis=-1, keepdims=True))
           - jnp.exp(jnp.sum(lp[2:3] * lp[3:4], axis=-1, keepdims=True)) + lam_init)
    lane = _lane_iota()
    chains = []
    for h in range(DIFF_HEADS):
        c = h * LANES
        q = q_ref[:, c:c + LANES].astype(F32)
        for qm in (jnp.where(lane < HEAD_DIM, q, 0.0), jnp.where(lane < HEAD_DIM, 0.0, q)):
            chains.append((k_ref.at[:, c:c + LANES], qm.astype(BF16), vt_ref.at[c:c + LANES, :]))
    outs = _attend_t(chains, k_ref.shape[0])
    for h in range(DIFF_HEADS):
        o = outs[2 * h] - lam * outs[2 * h + 1]
        o = o * lax.rsqrt(jnp.mean(o * o, axis=0, keepdims=True) + EPS) * (1.0 - lam_init)
        o_ref[:, h * LANES:(h + 1) * LANES] = (o.T * sg_ref[...]).astype(BF16)


def _gqa(a, avt, seq):
    r0, nb, L = seq
    tq = ATT_Q
    nq = L // tq
    return pl.pallas_call(
        _gqa_body, grid=(nb, nq),
        in_specs=[pl.BlockSpec((tq, A_Q_W), lambda b, i: (r0 // tq + b * nq + i, 0)),
                  pl.BlockSpec((L, LANES), lambda b, i: (r0 // L + b, A_Q_W // LANES)),
                  pl.BlockSpec((avt.shape[0], L), lambda b, i: (0, r0 // L + b))],
        out_specs=pl.BlockSpec((tq, A_Q_W), lambda b, i: (b * nq + i, 0)),
        out_shape=jax.ShapeDtypeStruct((nb * L, A_Q_W), BF16),
        compiler_params=_params("parallel", "parallel"), name="gqa_axial")(a, a, avt)


def _natten(bq, bias, seq):
    r0, nb, L = seq
    rows = L // GRID_W
    assert rows >= NA_KROWS and rows % NA_QROWS == 0
    tq = NA_QROWS * GRID_W
    nq = L // tq

    def bias_index(b, i):
        return (jnp.where(i == 0, 0, jnp.where(i == nq - 1, 2, 1)), 0, 0, 0)

    return pl.pallas_call(
        functools.partial(_natten_body, rows=rows), grid=(nb, nq),
        in_specs=[pl.BlockSpec((tq, B_W), lambda b, i: (r0 // tq + b * nq + i, 0)),
                  pl.BlockSpec((L, B_W), lambda b, i: (r0 // L + b, 1)),
                  pl.BlockSpec((L, B_W), lambda b, i: (r0 // L + b, 2)),
                  pl.BlockSpec((1,) + bias.shape[1:], bias_index)],
        out_specs=pl.BlockSpec((tq, B_W), lambda b, i: (b * nq + i, 0)),
        out_shape=jax.ShapeDtypeStruct((nb * L, B_W), BF16),
        compiler_params=_params("parallel", "arbitrary"), name="natten")(bq, bq, bq, bias)


def _diff(c, cvt, lam_rows, sub_gain, lam_init, seq):
    r0, nb, L = seq
    tq = ATT_Q
    nq = L // tq
    w = DIFF_HEADS * LANES
    ng = D_MODEL // w
    fixed = lambda b, h, i: (0, 0)
    return pl.pallas_call(
        functools.partial(_diff_body, lam_init=lam_init), grid=(nb, ng, nq),
        in_specs=[pl.BlockSpec((8, LANES), fixed), pl.BlockSpec((1, LANES), fixed),
                  pl.BlockSpec((tq, w), lambda b, h, i: (r0 // tq + b * nq + i, h)),
                  pl.BlockSpec((L, w), lambda b, h, i: (r0 // L + b, ng + h)),
                  pl.BlockSpec((w, L), lambda b, h, i: (h, r0 // L + b))],
        out_specs=pl.BlockSpec((tq, w), lambda b, h, i: (b * nq + i, h)),
        out_shape=jax.ShapeDtypeStruct((nb * L, D_MODEL), BF16),
        compiler_params=_params("parallel", "parallel", "parallel"), name="diff_attn",
    )(lam_rows, sub_gain, c, c, cvt)


def _route_body(lg_ref, meta_ref, cnt_ref, run_ref):
    @pl.when(pl.program_id(0) == 0)
    def _():
        run_ref[...] = jnp.zeros_like(run_ref)

    lg = lg_ref[...]
    tm = lg.shape[0]
    lane_i = _lane_iota(lg.shape)
    lane = lane_i.astype(F32)
    far = float(LANES)
    is_g = lane_i < N_GROUPS
    gl = jnp.where(is_g, lg, NEG)
    gmax = jnp.max(gl, axis=-1, keepdims=True)
    gidx = jnp.min(jnp.where(is_g & (gl == gmax), lane, far), axis=-1, keepdims=True)
    g_w = 1.0 / jnp.sum(jnp.where(is_g, jnp.exp(gl - gmax), 0.0), axis=-1, keepdims=True)
    eid_i = lane_i - N_GROUPS
    eid = eid_i.astype(F32)
    grp = lax.shift_right_arithmetic(eid_i, int(math.log2(EXPERTS_PER_GROUP))).astype(F32)
    in_grp = (eid_i >= 0) & (eid_i < N_EXPERTS) & (grp == gidx)
    el = jnp.where(in_grp, lg, NEG)
    e1 = jnp.max(el, axis=-1, keepdims=True)
    i1 = jnp.min(jnp.where(in_grp & (el == e1), eid, far), axis=-1, keepdims=True)
    rest = in_grp & (eid != i1)
    el2 = jnp.where(rest, lg, NEG)
    e2 = jnp.max(el2, axis=-1, keepdims=True)
    i2 = jnp.min(jnp.where(rest & (el2 == e2), eid, far), axis=-1, keepdims=True)
    t = jnp.exp(e2 - e1)
    w1 = g_w / (1.0 + t)
    w2 = g_w * t / (1.0 + t)
    pick = ((lane == i1) | (lane == i2))
    onehot = jnp.where(pick, 1.0, 0.0)
    r = lax.broadcasted_iota(I32, (tm, tm), 0)
    c = lax.broadcasted_iota(I32, (tm, tm), 1)
    before = jnp.where(c < r, 1.0, 0.0).astype(BF16)
    prefix = _dot(before, onehot.astype(BF16)) + run_ref[...]
    rank1 = jnp.sum(jnp.where(lane == i1, prefix, 0.0), axis=-1, keepdims=True)
    rank2 = jnp.sum(jnp.where(lane == i2, prefix, 0.0), axis=-1, keepdims=True)
    run = run_ref[...] + jnp.sum(onehot, axis=0, keepdims=True)
    run_ref[...] = run
    cnt_ref[...] = run
    meta = jnp.zeros(lg.shape, F32)
    for n, col in enumerate((i1, i2, w1, w2, rank1, rank2)):
        meta = jnp.where(lane_i == n, col, meta)
    meta_ref[...] = meta


def _route(logits):
    t = logits.shape[0]
    tm = ROUTE_TILE
    return pl.pallas_call(
        _route_body,
        grid=(t // tm,),
        in_specs=[pl.BlockSpec((tm, LANES), lambda i: (i, 0))],
        out_specs=[pl.BlockSpec((tm, LANES), lambda i: (i, 0)), pl.BlockSpec((1, LANES), lambda i: (0, 0))],
        out_shape=[jax.ShapeDtypeStruct((t, LANES), F32), jax.ShapeDtypeStruct((1, LANES), F32)],
        scratch_shapes=[pltpu.VMEM((1, LANES), F32)],
        compiler_params=_params("arbitrary"),
        name="route",
    )(logits)


def _row_copy(src, i, dst, j, sem):
    return pltpu.make_async_copy(src.at[pl.ds(i, 1), :], dst.at[pl.ds(j, 1), :], sem)


def _dispatch_body(dest_ref, hn_ref, xs_in_ref, xs_ref, sem):
    del xs_in_ref
    tm = hn_ref.shape[0]

    def issue(k0, carry):
        for u in range(MOVE_UNROLL):
            k = k0 * MOVE_UNROLL + u
            _row_copy(hn_ref, k, xs_ref, dest_ref[0, 0, k], sem).start()
            _row_copy(hn_ref, k, xs_ref, dest_ref[0, 0, tm + k], sem).start()
        return carry

    lax.fori_loop(0, tm // MOVE_UNROLL, issue, 0)
    for _ in range(2):
        pltpu.make_async_copy(hn_ref, xs_ref.at[pl.ds(0, tm), :], sem).wait()


def _dispatch(hn, dest, n_rows):
    t = hn.shape[0]
    tm = MOVE_TILE
    zeros = jnp.zeros((n_rows, D_MODEL), F32)
    return pl.pallas_call(
        _dispatch_body,
        grid=(t // tm,),
        in_specs=[pl.BlockSpec((1, 1, 2 * tm), lambda i: (i, 0, 0), memory_space=pltpu.SMEM),
                  pl.BlockSpec((tm, D_MODEL), lambda i: (i, 0)),
                  pl.BlockSpec(memory_space=pl.ANY)],
        out_specs=pl.BlockSpec(memory_space=pl.ANY),
        out_shape=jax.ShapeDtypeStruct((n_rows, D_MODEL), F32),
        scratch_shapes=[pltpu.SemaphoreType.DMA],
        input_output_aliases={2: 0},
        compiler_params=_params("arbitrary"),
        name="moe_dispatch",
    )(dest, hn, zeros)


def _expert_body(te_ref, used_ref, x_ref, wg_ref, wu_ref, wd_ref, y_ref):
    i = pl.program_id(0)

    @pl.when(i < used_ref[0])
    def _():
        xb = x_ref[...].astype(BF16)
        g = _dot(xb, wg_ref[0])
        u = _dot(xb, wu_ref[0])
        h = (g / (1.0 + jnp.exp(-g)) * u).astype(BF16)
        y_ref[...] = _dot(h, wd_ref[0])

    @pl.when(i >= used_ref[0])
    def _():
        y_ref[...] = jnp.zeros_like(y_ref)


def _experts(xs, tile_expert, n_used, w_gate, w_up, w_down):
    n_rows = xs.shape[0]
    tm = EXPERT_TILE
    grid_spec = pltpu.PrefetchScalarGridSpec(
        num_scalar_prefetch=2,
        grid=(n_rows // tm,),
        in_specs=[pl.BlockSpec((tm, D_MODEL), lambda i, te, nu: (i, 0)),
                  pl.BlockSpec((1, D_MODEL, D_EXPERT), lambda i, te, nu: (te[i], 0, 0)),
                  pl.BlockSpec((1, D_MODEL, D_EXPERT), lambda i, te, nu: (te[i], 0, 0)),
                  pl.BlockSpec((1, D_EXPERT, D_MODEL), lambda i, te, nu: (te[i], 0, 0))],
        out_specs=pl.BlockSpec((tm, D_MODEL), lambda i, te, nu: (i, 0)),
    )
    return pl.pallas_call(
        _expert_body,
        grid_spec=grid_spec,
        out_shape=jax.ShapeDtypeStruct((n_rows, D_MODEL), F32),
        compiler_params=_params("arbitrary"),
        name="moe_experts",
    )(tile_expert, n_used, xs, w_gate, w_up, w_down)


def _combine_body(dest_ref, x_ref, meta_ref, ys_ref, *rest, final):
    if final:
        g_ref, o_ref, ya_ref, yb_ref, sem = rest
    else:
        o_ref, ya_ref, yb_ref, sem = rest
    tm = x_ref.shape[0]

    def issue(k0, carry):
        for u in range(MOVE_UNROLL):
            k = k0 * MOVE_UNROLL + u
            _row_copy(ys_ref, dest_ref[0, 0, k], ya_ref, k, sem).start()
            _row_copy(ys_ref, dest_ref[0, 0, tm + k], yb_ref, k, sem).start()
        return carry

    lax.fori_loop(0, tm // MOVE_UNROLL, issue, 0)
    for buf in (ya_ref, yb_ref):
        pltpu.make_async_copy(ys_ref.at[pl.ds(0, tm), :], buf, sem).wait()
    meta = meta_ref[...]
    x = x_ref[...] + meta[:, 2:3] * ya_ref[...] + meta[:, 3:4] * yb_ref[...]
    o_ref[...] = _rms(x, g_ref[...]) if final else x


def _combine(x1, meta, dest, ys, r0, rows, final_gain=None):
    tm = MOVE_TILE
    first = r0 // tm
    final = final_gain is not None
    tile = lambda i: (first + i, 0)
    in_specs = [pl.BlockSpec((1, 1, 2 * tm), lambda i: (first + i, 0, 0), memory_space=pltpu.SMEM),
                pl.BlockSpec((tm, D_MODEL), tile),
                pl.BlockSpec((tm, LANES), tile),
                pl.BlockSpec(memory_space=pl.ANY)]
    args = [dest, x1, meta, ys]
    if final:
        in_specs.append(pl.BlockSpec((1, D_MODEL), lambda i: (0, 0)))
        args.append(final_gain.reshape(1, D_MODEL))
    return pl.pallas_call(
        functools.partial(_combine_body, final=final),
        grid=(rows // tm,),
        in_specs=in_specs,
        out_specs=pl.BlockSpec((tm, D_MODEL), lambda i: (i, 0)),
        out_shape=jax.ShapeDtypeStruct((rows, D_MODEL), F32),
        scratch_shapes=[pltpu.VMEM((tm, D_MODEL), F32), pltpu.VMEM((tm, D_MODEL), F32),
                        pltpu.SemaphoreType.DMA],
        compiler_params=_params("arbitrary"),
        name="moe_combine_final" if final else "moe_combine",
    )(*args)


def _moe(x1, hn, logits, w_gate, w_up, w_down, final=None):
    t = x1.shape[0]
    tm = MOVE_TILE
    meta, counts = _route(logits)
    counts = counts[0, :N_EXPERTS].astype(I32)
    padded = (counts + EXPERT_TILE - 1) // EXPERT_TILE * EXPERT_TILE
    ends = jnp.cumsum(padded)
    offsets = ends - padded
    n_rows = 2 * t + N_EXPERTS * EXPERT_TILE
    n_tiles = n_rows // EXPERT_TILE
    n_used = (ends[-1] // EXPERT_TILE).astype(I32)
    tile_start = jnp.minimum(jnp.arange(n_tiles, dtype=I32), n_used - 1) * EXPERT_TILE
    tile_expert = jnp.sum((ends[None, :] <= tile_start[:, None]).astype(I32), axis=1)
    experts = jnp.arange(N_EXPERTS, dtype=I32)[None, :]

    def position(e_col, rank_col):
        e = meta[:, e_col].astype(I32)[:, None]
        return jnp.sum(jnp.where(e == experts, offsets[None, :], 0), axis=1) + meta[:, rank_col].astype(I32)

    dest = jnp.concatenate([position(0, 4).reshape(t // tm, 1, tm), position(1, 5).reshape(t // tm, 1, tm)], axis=-1)
    xs = _dispatch(hn, dest, n_rows)
    ys = _experts(xs, tile_expert, n_used.reshape(1), w_gate, w_up, w_down)
    if final is None:
        return _combine(x1, meta, dest, ys, 0, t)
    gain, ranges = final
    return [_combine(x1, meta, dest, ys, r0, rows, gain) for r0, rows in ranges]


def kernel(x_prompt, x_sample, norm_mix, norm_ffn, norm_final, w_in_even, q_gain, k_gain, rpb, w_out_even,
           w_in_odd, lam_q1, lam_k1, lam_q2, lam_k2, subln_gain, w_out_odd,
           w_rg, b_rg, w_re, b_re, w_gate, w_up, w_down):
    bp, lp, d = x_prompt.shape
    bs, ls, _ = x_sample.shape
    assert d == D_MODEL and (bp * lp) % ls == 0 and lp % MOVE_TILE == 0 and ls % MOVE_TILE == 0
    tp, ts = bp * lp, bs * ls
    t = tp + ts
    seqs = ((0, bp, lp), (tp, bs, ls))
    x = [x_prompt.reshape(tp, d), x_sample.reshape(ts, d)]
    max_len = max(lp, ls)
    depth = norm_mix.shape[0]

    def router(i):
        w = jnp.zeros((D_MODEL, LANES), F32)
        w = w.at[:, :N_GROUPS].set(w_rg[i]).at[:, N_GROUPS:N_GROUPS + N_EXPERTS].set(w_re[i])
        b = jnp.zeros((1, LANES), F32)
        b = b.at[0, :N_GROUPS].set(b_rg[i]).at[0, N_GROUPS:N_GROUPS + N_EXPERTS].set(b_re[i])
        w_hi = w.astype(BF16)
        w_lo = (w - w_hi.astype(F32)).astype(BF16)
        return jnp.concatenate([w_hi, w_lo], axis=1), b

    for i in range(depth):
        j = i // 2
        if i % 2 == 0:
            head_gain = jnp.stack([jnp.tile(q_gain[j], 2), jnp.tile(k_gain[j], 2)])
            a, b, avt = _inproj(x, norm_mix[i], w_in_even[j].astype(BF16), _axial_tables(max_len), seqs, head_gain)
            bias = _natten_bias(rpb[j])
            pieces = [[_gqa(a, avt, seq), _natten(b, bias, seq)] for seq in seqs]
            w_out = w_out_even[j]
        else:
            c, cvt = _inproj(x, norm_mix[i], w_in_odd[j].astype(BF16), _partial_tables(max_len), seqs)
            lam_init = 0.8 - 0.6 * math.exp(-0.3 * i)
            lam_rows = jnp.zeros((8, LANES), F32).at[:4, :HEAD_DIM].set(
                jnp.stack([lam_q1[j], lam_k1[j], lam_q2[j], lam_k2[j]]))
            pieces = [[_diff(c, cvt, lam_rows, subln_gain[j].reshape(1, LANES), lam_init, seq)] for seq in seqs]
            w_out = w_out_odd[j]
        w_router, b_router = router(i)
        x1, hn, logits = _outproj(pieces, w_out.astype(BF16), x, norm_ffn[i], w_router, b_router)
        final = (norm_final, ((0, tp), (tp, ts))) if i == depth - 1 else None
        x = _moe(x1, hn, logits, w_gate[i].astype(BF16), w_up[i].astype(BF16), w_down[i].astype(BF16), final)
        x = x if final else [x]
    y_prompt, y_sample = x
    return (y_prompt.reshape(bp, lp, d), y_sample.reshape(bs, ls, d))
```

```python
import functools
import math

import numpy as np
import jax
import jax.numpy as jnp
from jax import lax
from jax.experimental import pallas as pl
from jax.experimental.pallas import tpu as pltpu

F32 = jnp.float32
BF16 = jnp.bfloat16
I32 = jnp.int32

D_MODEL = 1024
HEAD_DIM = 64
GRID_W = 64
EPS = 1e-6
LANES = 128
MXU_W = 256
A_Q_W, A_KV_W, B_W = 512, 128, 512
EVEN_IN = A_Q_W + 2 * A_KV_W + 3 * B_W
ODD_IN = 3 * D_MODEL
AXIAL_THETA = 10000.0
ROPE_THETA = 500000.0
ROPE_DIMS = HEAD_DIM // 4
NA_ROWS, NA_COLS = 8, 16
NA_QROWS = 4
NA_KROWS = NA_QROWS + NA_ROWS
N_GROUPS, EXPERTS_PER_GROUP = 4, 8
N_EXPERTS = N_GROUPS * EXPERTS_PER_GROUP
D_EXPERT = 512
LOG2E = math.log2(math.e)
SCALE = HEAD_DIM ** -0.5 * LOG2E
NEG = -1e30

VMEM_LIMIT_BYTES = 56 * 1024 * 1024
TOK_TILE = 512
ROUTE_TILE = 512
META_ROWS = 8
DISPATCH_TILE = 2048
MOVE_TILE = 1024
MOVE_UNROLL = 16
EXPERT_TILE = 512
ATT_Q = 256
DIFF_HEADS = 4
KEY_CHUNK = 4096
SCORES_AHEAD = 4

def _params(*sem):
    return pltpu.CompilerParams(dimension_semantics=sem, vmem_limit_bytes=VMEM_LIMIT_BYTES)


def _lane_iota(shape=(1, LANES)):
    return lax.broadcasted_iota(I32, shape, len(shape) - 1)


def _rms(x, gain):
    return x * lax.rsqrt(jnp.mean(x * x, axis=-1, keepdims=True) + EPS) * gain


def _dot(a, b):
    return jnp.dot(a, b, preferred_element_type=F32)


def _dot_nt(a, b):
    return lax.dot_general(a, b, (((1,), (1,)), ((), ())), preferred_element_type=F32)


def _rope(y, tab_ref, shift):
    return (y * tab_ref[0] + pltpu.roll(y, LANES - shift, 1) * tab_ref[1]
            + pltpu.roll(y, shift, 1) * tab_ref[2])


def _rope_tables(angles, n):
    cos = jnp.cos(angles)
    sin = jnp.sin(angles)
    low = jnp.asarray((np.arange(HEAD_DIM) % n) < n // 2)
    tab = jnp.stack([cos, jnp.where(low, -sin, 0.0), jnp.where(low, 0.0, sin)])
    return jnp.concatenate([tab, tab], axis=-1).astype(F32)


def _axial_tables(max_len):
    pos = jnp.arange(max_len)
    row = (pos // GRID_W).astype(F32)
    col = (pos % GRID_W).astype(F32)
    half = HEAD_DIM // 2
    inv = AXIAL_THETA ** (-jnp.arange(0, half, 2, dtype=F32) / half)
    inv2 = jnp.concatenate([inv, inv])
    ang = jnp.concatenate([row[:, None] * inv2[None], col[:, None] * inv2[None]], axis=-1)
    return _rope_tables(ang, half)


def _partial_tables(max_len):
    pos = jnp.arange(max_len).astype(F32)
    inv = ROPE_THETA ** (-jnp.arange(0, ROPE_DIMS, 2, dtype=F32) / ROPE_DIMS)
    inv2 = jnp.concatenate([inv, inv])
    ang = pos[:, None] * inv2[None]
    tab = _rope_tables(jnp.concatenate([ang, jnp.zeros((max_len, HEAD_DIM - ROPE_DIMS), F32)], -1), ROPE_DIMS)
    keep = jnp.asarray(np.tile(np.arange(HEAD_DIM) < ROPE_DIMS, 2))
    return jnp.stack([jnp.where(keep, tab[0], 1.0), jnp.where(keep, tab[1], 0.0), jnp.where(keep, tab[2], 0.0)])


def _natten_bias(rpb):
    c, kc = np.arange(GRID_W)[:, None], np.arange(GRID_W)[None, :]
    cs = np.clip(c - NA_COLS // 2, 0, GRID_W - NA_COLS)
    col_ok = (kc >= cs) & (kc < cs + NA_COLS)
    col_pick = (kc - c + NA_COLS - 1)[None] == np.arange(2 * NA_COLS - 1)[:, None, None]
    by_col = jnp.einsum("hdm,mcq->hdcq", rpb.astype(F32), jnp.asarray(col_pick, F32),
                        precision=lax.Precision.HIGHEST)
    rl, ki = np.arange(NA_QROWS)[:, None], np.arange(NA_KROWS)[None, :]
    out = []
    for delta, first in ((0, 0 * rl), (NA_QROWS, rl), (2 * NA_QROWS, 0 * rl + NA_QROWS)):
        row_ok = (ki >= first) & (ki < first + NA_ROWS)
        row_pick = ((ki - delta - rl + NA_ROWS - 1)[None] == np.arange(2 * NA_ROWS - 1)[:, None, None]) & row_ok
        b = jnp.einsum("drk,hdcq->hrckq", jnp.asarray(row_pick, F32), by_col, precision=lax.Precision.HIGHEST)
        ok = row_ok[:, None, :, None] & col_ok[None, :, None, :]
        b = jnp.where(jnp.asarray(ok)[None], b * LOG2E, NEG)
        out.append(b.reshape(rpb.shape[0], NA_QROWS * GRID_W, NA_KROWS * GRID_W))
    return jnp.stack(out)


def _rows_specs(parts, tm):
    if len(parts) == 1:
        return [pl.BlockSpec((tm, D_MODEL), lambda i: (i, 0))], 0
    first = parts[0].shape[0] // tm
    return [pl.BlockSpec((tm, D_MODEL), lambda i: (jnp.minimum(i, first - 1), 0)),
            pl.BlockSpec((tm, D_MODEL), lambda i: (jnp.maximum(i - first, 0), 0))], first


def _rows_value(refs, first_tiles):
    if len(refs) == 1:
        return refs[0][...]
    return jnp.where(pl.program_id(0) < first_tiles, refs[0][...], refs[1][...])


def _inproj_even_body(*refs, n_x, first_tiles):
    g_ref, w_ref, tab_ref, hg_ref, oa_ref, ob_ref, avt_ref = refs[n_x:]
    hn = _rms(_rows_value(refs[:n_x], first_tiles), g_ref[...]).astype(BF16)
    low = _lane_iota() < HEAD_DIM

    def head_norm_rope(y, gain, scale):
        ss = y * y
        s_lo = jnp.sum(jnp.where(low, ss, 0.0), axis=-1, keepdims=True)
        s_hi = jnp.sum(jnp.where(low, 0.0, ss), axis=-1, keepdims=True)
        y = y * lax.rsqrt(jnp.where(low, s_lo, s_hi) * (1.0 / HEAD_DIM) + EPS) * gain
        return _rope(y, tab_ref, HEAD_DIM // 4) * scale

    for c in range(0, A_Q_W, MXU_W):
        y = _dot(hn, w_ref[:, c:c + MXU_W])
        for h in range(0, MXU_W, LANES):
            oa_ref[:, c + h:c + h + LANES] = head_norm_rope(y[:, h:h + LANES], hg_ref[0:1, :], SCALE).astype(BF16)
    y = _dot(hn, w_ref[:, A_Q_W:A_Q_W + MXU_W])
    oa_ref[:, A_Q_W:A_Q_W + LANES] = head_norm_rope(y[:, :LANES], hg_ref[1:2, :], 1.0).astype(BF16)
    avt_ref[...] = y[:, LANES:].T.astype(BF16)
    c0 = A_Q_W + 2 * A_KV_W
    for c in range(0, 3 * B_W, MXU_W):
        y = _dot(hn, w_ref[:, c0 + c:c0 + c + MXU_W])
        ob_ref[:, c:c + MXU_W] = (y * SCALE if c < B_W else y).astype(BF16)


def _inproj_odd_body(*refs, n_x, first_tiles):
    g_ref, w_ref, tab_ref, oc_ref, cvt_ref = refs[n_x:]
    hn = _rms(_rows_value(refs[:n_x], first_tiles), g_ref[...]).astype(BF16)
    for c in range(0, 2 * D_MODEL, MXU_W):
        y = _dot(hn, w_ref[:, c:c + MXU_W])
        for h in range(0, MXU_W, LANES):
            z = _rope(y[:, h:h + LANES], tab_ref, ROPE_DIMS // 2)
            oc_ref[:, c + h:c + h + LANES] = (z * SCALE if c < D_MODEL else z).astype(BF16)
    for c in range(0, D_MODEL, MXU_W):
        cvt_ref[c:c + MXU_W, :] = _dot(hn, w_ref[:, 2 * D_MODEL + c:2 * D_MODEL + c + MXU_W]).T.astype(BF16)


def _tab_index(seqs, tm):
    (r0, _, l0), (r1, _, l1) = seqs

    def index(i):
        return (0, jnp.where(i < r1 // tm, (i - r0 // tm) % (l0 // tm), (i - r1 // tm) % (l1 // tm)), 0)
    return index


def _inproj(x_parts, gain, w, tab, seqs, head_gain=None):
    t = sum(p.shape[0] for p in x_parts)
    tm = TOK_TILE
    even = head_gain is not None
    x_specs, first_tiles = _rows_specs(x_parts, tm)
    in_specs = x_specs + [pl.BlockSpec((1, D_MODEL), lambda i: (0, 0)),
                          pl.BlockSpec(w.shape, lambda i: (0, 0)),
                          pl.BlockSpec((3, tm, LANES), _tab_index(seqs, tm))]
    args = list(x_parts) + [gain.reshape(1, D_MODEL), w, tab]
    if even:
        in_specs.append(pl.BlockSpec((2, LANES), lambda i: (0, 0)))
        args.append(head_gain)
        widths, t_width = (A_Q_W + A_KV_W, 3 * B_W), A_KV_W
        body = _inproj_even_body
    else:
        widths, t_width = (2 * D_MODEL,), D_MODEL
        body = _inproj_odd_body
    return pl.pallas_call(
        functools.partial(body, n_x=len(x_parts), first_tiles=first_tiles),
        grid=(t // tm,),
        in_specs=in_specs,
        out_specs=[pl.BlockSpec((tm, n), lambda i: (i, 0)) for n in widths]
        + [pl.BlockSpec((t_width, tm), lambda i: (0, i))],
        out_shape=[jax.ShapeDtypeStruct((t, n), BF16) for n in widths]
        + [jax.ShapeDtypeStruct((t_width, t), BF16)],
        compiler_params=_params("parallel"),
        name="inproj_even" if even else "inproj_odd",
    )(*args)


def _outproj_body(*refs, n_pieces, n_x, first_tiles):
    o0, o1 = refs[:n_pieces], refs[n_pieces:2 * n_pieces]
    x_refs = refs[2 * n_pieces:2 * n_pieces + n_x]
    w_ref, g_ref, wr_ref, br_ref, x1_ref, hn_ref, lg_ref = refs[2 * n_pieces + n_x:]
    in_first = pl.program_id(0) < first_tiles
    x1 = _rows_value(x_refs, first_tiles)
    c = 0
    for a_ref, b_ref in zip(o0, o1):
        n = a_ref.shape[1]
        x1 = x1 + _dot(jnp.where(in_first, a_ref[...], b_ref[...]), w_ref[c:c + n, :])
        c += n
    x1_ref[...] = x1
    hn = _rms(x1, g_ref[...])
    hn_ref[...] = hn
    hi = hn.astype(BF16)
    lo = (hn - hi.astype(F32)).astype(BF16)
    both = _dot(hi, wr_ref[...])
    lg_ref[...] = both[:, :LANES] + both[:, LANES:] + _dot(lo, wr_ref[:, :LANES]) + br_ref[...]


def _outproj(pieces, w, x_parts, gain, w_router, b_router):
    t = sum(p.shape[0] for p in x_parts)
    tm = TOK_TILE
    row = lambda i: (i, 0)
    fixed = lambda i: (0, 0)
    first_tiles = pieces[0][0].shape[0] // tm
    piece_specs = (
        [pl.BlockSpec((tm, p.shape[1]), lambda i: (jnp.minimum(i, first_tiles - 1), 0)) for p in pieces[0]]
        + [pl.BlockSpec((tm, p.shape[1]), lambda i: (jnp.maximum(i - first_tiles, 0), 0)) for p in pieces[1]])
    x_specs, x_first = _rows_specs(x_parts, tm)
    assert len(x_parts) == 1 or x_first == first_tiles
    return pl.pallas_call(
        functools.partial(_outproj_body, n_pieces=len(pieces[0]), n_x=len(x_parts), first_tiles=first_tiles),
        grid=(t // tm,),
        in_specs=piece_specs + x_specs + [
            pl.BlockSpec((D_MODEL, D_MODEL), fixed), pl.BlockSpec((1, D_MODEL), fixed),
            pl.BlockSpec((D_MODEL, 2 * LANES), fixed), pl.BlockSpec((1, LANES), fixed)],
        out_specs=[pl.BlockSpec((tm, D_MODEL), row), pl.BlockSpec((tm, D_MODEL), row),
                   pl.BlockSpec((tm, LANES), row)],
        out_shape=[jax.ShapeDtypeStruct((t, D_MODEL), F32), jax.ShapeDtypeStruct((t, D_MODEL), F32),
                   jax.ShapeDtypeStruct((t, LANES), F32)],
        compiler_params=_params("parallel"),
        name="outproj_router",
    )(*pieces[0], *pieces[1], *x_parts, w, gain.reshape(1, D_MODEL), w_router, b_router)


def _softmax_pv(s, v):
    m = jnp.max(s, axis=-1, keepdims=True)
    p = jnp.exp2(s - m)
    l = jnp.sum(p, axis=-1, keepdims=True)
    return _dot(p.astype(BF16), v) / l


def _attend_t(chains, n_keys):
    tq = chains[0][1].shape[0]
    state = [(jnp.full((1, tq), NEG, F32), jnp.zeros((1, tq), F32), None) for _ in chains]
    chunk = min(KEY_CHUNK, n_keys)
    units = [(n, slice(t * chunk, (t + 1) * chunk)) for t in range(n_keys // chunk) for n in range(len(chains))]

    def scores(unit):
        n, rows = unit
        k_ref, q, _ = chains[n]
        return _dot_nt(k_ref[rows, :], q)

    queue = [scores(unit) for unit in units[:SCORES_AHEAD]]
    for u, (n, rows) in enumerate(units):
        s = queue.pop(0)
        if u + SCORES_AHEAD < len(units):
            queue.append(scores(units[u + SCORES_AHEAD]))
        m, l, acc = state[n]
        m_new = jnp.maximum(m, jnp.max(s, axis=0, keepdims=True))
        alpha = jnp.exp2(m - m_new)
        p = jnp.exp2(s - m_new)
        l = alpha * l + jnp.sum(p, axis=0, keepdims=True)
        pv = _dot(chains[n][2][:, rows], p.astype(BF16))
        state[n] = (m_new, l, pv if acc is None else alpha * acc + pv)
    return [acc / l for _, l, acc in state]


def _gqa_body(q_ref, k_ref, vt_ref, o_ref):
    lane = _lane_iota()
    chains = []
    for j in range(A_Q_W // LANES):
        g = j // 2
        keep = (lane >= g * HEAD_DIM) & (lane < (g + 1) * HEAD_DIM)
        q2 = q_ref[:, j * LANES:(j + 1) * LANES].astype(F32)
        for par in range(2):
            qh = q2 if par == g else pltpu.roll(q2, HEAD_DIM, 1)
            qh = jnp.where(keep, qh, 0.0).astype(BF16)
            chains.append((k_ref, qh, vt_ref.at[g * HEAD_DIM:(g + 1) * HEAD_DIM, :]))
    outs = _attend_t(chains, k_ref.shape[0])
    for j in range(A_Q_W // LANES):
        o_ref[:, j * LANES:(j + 1) * LANES] = jnp.concatenate(outs[2 * j:2 * j + 2], axis=0).T.astype(BF16)


def _natten_body(q_ref, k_ref, v_ref, bias_ref, o_ref, *, rows):
    rb = pl.program_id(1)
    start = pl.multiple_of(jnp.clip(rb * NA_QROWS - NA_ROWS // 2, 0, rows - NA_KROWS) * GRID_W, GRID_W)
    lane = _lane_iota()
    nk = NA_KROWS * GRID_W
    heads = []
    for j in range(B_W // LANES):
        q2 = q_ref[:, j * LANES:(j + 1) * LANES].astype(F32)
        for keep in (lane < HEAD_DIM, lane >= HEAD_DIM):
            heads.append((jnp.where(keep, q2, 0.0).astype(BF16), j))

    def scores(h):
        qh, j = heads[h]
        return _dot_nt(qh, k_ref[pl.ds(start, nk), j * LANES:(j + 1) * LANES]) + bias_ref[0, h]

    res = []
    s_next = scores(0)
    for h, (_, j) in enumerate(heads):
        s = s_next
        if h + 1 < len(heads):
            s_next = scores(h + 1)
        res.append(_softmax_pv(s, v_ref[pl.ds(start, nk), j * LANES:(j + 1) * LANES]))
    for j in range(B_W // LANES):
        o_ref[:, j * LANES:(j + 1) * LANES] = jnp.where(lane < HEAD_DIM, res[2 * j], res[2 * j + 1]).astype(BF16)


def _diff_body(lam_ref, sg_ref, q_ref, k_ref, vt_ref, o_ref, *, lam_init):
    lp = lam_ref[...]
    lam = (jnp.exp(jnp.sum(lp[0:1] * lp[1:2], axis=-1, keepdims=True))
           - jnp.exp(jnp.sum(lp[2:3] * lp[3:4], axis=-1, keepdims=True)) + lam_init)
    lane = _lane_iota()
    chains = []
    for h in range(DIFF_HEADS):
        c = h * LANES
        q = q_ref[:, c:c + LANES].astype(F32)
        for qm in (jnp.where(lane < HEAD_DIM, q, 0.0), jnp.where(lane < HEAD_DIM, 0.0, q)):
            chains.append((k_ref.at[:, c:c + LANES], qm.astype(BF16), vt_ref.at[c:c + LANES, :]))
    outs = _attend_t(chains, k_ref.shape[0])
    for h in range(DIFF_HEADS):
        o = outs[2 * h] - lam * outs[2 * h + 1]
        o = o * lax.rsqrt(jnp.mean(o * o, axis=0, keepdims=True) + EPS) * (1.0 - lam_init)
        o_ref[:, h * LANES:(h + 1) * LANES] = (o.T * sg_ref[...]).astype(BF16)


def _gqa(a, avt, seq):
    r0, nb, L = seq
    tq = ATT_Q
    nq = L // tq
    return pl.pallas_call(
        _gqa_body, grid=(nb, nq),
        in_specs=[pl.BlockSpec((tq, A_Q_W), lambda b, i: (r0 // tq + b * nq + i, 0)),
                  pl.BlockSpec((L, LANES), lambda b, i: (r0 // L + b, A_Q_W // LANES)),
                  pl.BlockSpec((avt.shape[0], L), lambda b, i: (0, r0 // L + b))],
        out_specs=pl.BlockSpec((tq, A_Q_W), lambda b, i: (b * nq + i, 0)),
        out_shape=jax.ShapeDtypeStruct((nb * L, A_Q_W), BF16),
        compiler_params=_params("parallel", "parallel"), name="gqa_axial")(a, a, avt)


def _natten(bq, bias, seq):
    r0, nb, L = seq
    rows = L // GRID_W
    assert rows >= NA_KROWS and rows % NA_QROWS == 0
    tq = NA_QROWS * GRID_W
    nq = L // tq

    def bias_index(b, i):
        return (jnp.where(i == 0, 0, jnp.where(i == nq - 1, 2, 1)), 0, 0, 0)

    return pl.pallas_call(
        functools.partial(_natten_body, rows=rows), grid=(nb, nq),
        in_specs=[pl.BlockSpec((tq, B_W), lambda b, i: (r0 // tq + b * nq + i, 0)),
                  pl.BlockSpec((L, B_W), lambda b, i: (r0 // L + b, 1)),
                  pl.BlockSpec((L, B_W), lambda b, i: (r0 // L + b, 2)),
                  pl.BlockSpec((1,) + bias.shape[1:], bias_index)],
        out_specs=pl.BlockSpec((tq, B_W), lambda b, i: (b * nq + i, 0)),
        out_shape=jax.ShapeDtypeStruct((nb * L, B_W), BF16),
        compiler_params=_params("parallel", "arbitrary"), name="natten")(bq, bq, bq, bias)


def _diff(c, cvt, lam_rows, sub_gain, lam_init, seq):
    r0, nb, L = seq
    tq = ATT_Q
    nq = L // tq
    w = DIFF_HEADS * LANES
    ng = D_MODEL // w
    fixed = lambda b, h, i: (0, 0)
    return pl.pallas_call(
        functools.partial(_diff_body, lam_init=lam_init), grid=(nb, ng, nq),
        in_specs=[pl.BlockSpec((8, LANES), fixed), pl.BlockSpec((1, LANES), fixed),
                  pl.BlockSpec((tq, w), lambda b, h, i: (r0 // tq + b * nq + i, h)),
                  pl.BlockSpec((L, w), lambda b, h, i: (r0 // L + b, ng + h)),
                  pl.BlockSpec((w, L), lambda b, h, i: (h, r0 // L + b))],
        out_specs=pl.BlockSpec((tq, w), lambda b, h, i: (b * nq + i, h)),
        out_shape=jax.ShapeDtypeStruct((nb * L, D_MODEL), BF16),
        compiler_params=_params("parallel", "parallel", "parallel"), name="diff_attn",
    )(lam_rows, sub_gain, c, c, cvt)


def _route_body(lg_ref, meta_ref, idx_ref, cnt_ref, run_ref):
    @pl.when(pl.program_id(0) == 0)
    def _():
        run_ref[...] = jnp.zeros_like(run_ref)

    lg = lg_ref[...]
    tm = lg.shape[0]
    lane_i = _lane_iota(lg.shape)
    lane = lane_i.astype(F32)
    far = float(LANES)
    is_g = lane_i < N_GROUPS
    gl = jnp.where(is_g, lg, NEG)
    gmax = jnp.max(gl, axis=-1, keepdims=True)
    gidx = jnp.min(jnp.where(is_g & (gl == gmax), lane, far), axis=-1, keepdims=True)
    g_w = 1.0 / jnp.sum(jnp.where(is_g, jnp.exp(gl - gmax), 0.0), axis=-1, keepdims=True)
    eid_i = lane_i - N_GROUPS
    eid = eid_i.astype(F32)
    grp = lax.shift_right_arithmetic(eid_i, int(math.log2(EXPERTS_PER_GROUP))).astype(F32)
    in_grp = (eid_i >= 0) & (eid_i < N_EXPERTS) & (grp == gidx)
    el = jnp.where(in_grp, lg, NEG)
    e1 = jnp.max(el, axis=-1, keepdims=True)
    i1 = jnp.min(jnp.where(in_grp & (el == e1), eid, far), axis=-1, keepdims=True)
    rest = in_grp & (eid != i1)
    el2 = jnp.where(rest, lg, NEG)
    e2 = jnp.max(el2, axis=-1, keepdims=True)
    i2 = jnp.min(jnp.where(rest & (el2 == e2), eid, far), axis=-1, keepdims=True)
    t = jnp.exp(e2 - e1)
    w1 = g_w / (1.0 + t)
    w2 = g_w * t / (1.0 + t)
    pick = ((lane == i1) | (lane == i2))
    onehot = jnp.where(pick, 1.0, 0.0)
    r = lax.broadcasted_iota(I32, (tm, tm), 0)
    c = lax.broadcasted_iota(I32, (tm, tm), 1)
    before = jnp.where(c < r, 1.0, 0.0).astype(BF16)
    prefix = _dot(before, onehot.astype(BF16)) + run_ref[...]
    rank1 = jnp.sum(jnp.where(lane == i1, prefix, 0.0), axis=-1, keepdims=True)
    rank2 = jnp.sum(jnp.where(lane == i2, prefix, 0.0), axis=-1, keepdims=True)
    run = run_ref[...] + jnp.sum(onehot, axis=0, keepdims=True)
    run_ref[...] = run
    cnt_ref[...] = run
    meta = jnp.zeros(lg.shape, F32)
    for n, col in enumerate((i1, i2, w1, w2, rank1, rank2)):
        meta = jnp.where(lane_i == n, col, meta)
    meta_ref[...] = meta
    idx_ref[0] = meta.T[:META_ROWS, :]


def _route(logits):
    t = logits.shape[0]
    tm = ROUTE_TILE
    return pl.pallas_call(
        _route_body,
        grid=(t // tm,),
        in_specs=[pl.BlockSpec((tm, LANES), lambda i: (i, 0))],
        out_specs=[pl.BlockSpec((tm, LANES), lambda i: (i, 0)),
                   pl.BlockSpec((1, META_ROWS, tm), lambda i: (i, 0, 0)),
                   pl.BlockSpec((1, LANES), lambda i: (0, 0))],
        out_shape=[jax.ShapeDtypeStruct((t, LANES), F32),
                   jax.ShapeDtypeStruct((t // tm, META_ROWS, tm), F32),
                   jax.ShapeDtypeStruct((1, LANES), F32)],
        scratch_shapes=[pltpu.VMEM((1, LANES), F32)],
        compiler_params=_params("arbitrary"),
        name="route",
    )(logits)


def _row_copy(src, i, dst, j, sem):
    return pltpu.make_async_copy(src.at[pl.ds(i, 1), :], dst.at[pl.ds(j, 1), :], sem)


def _dispatch_body(dest_ref, hn_ref, xs_in_ref, xs_ref, sem):
    del xs_in_ref
    tm = hn_ref.shape[0]

    def issue(k0, carry):
        for u in range(MOVE_UNROLL):
            k = k0 * MOVE_UNROLL + u
            _row_copy(hn_ref, k, xs_ref, dest_ref[0, 0, k], sem).start()
            _row_copy(hn_ref, k, xs_ref, dest_ref[0, 0, tm + k], sem).start()
        return carry

    lax.fori_loop(0, tm // MOVE_UNROLL, issue, 0)
    for _ in range(2):
        pltpu.make_async_copy(hn_ref, xs_ref.at[pl.ds(0, tm), :], sem).wait()


def _dispatch(hn, dest, n_rows):
    t = hn.shape[0]
    tm = DISPATCH_TILE
    zeros = jnp.zeros((n_rows, D_MODEL), F32)
    return pl.pallas_call(
        _dispatch_body,
        grid=(t // tm,),
        in_specs=[pl.BlockSpec((1, 1, 2 * tm), lambda i: (i, 0, 0), memory_space=pltpu.SMEM),
                  pl.BlockSpec((tm, D_MODEL), lambda i: (i, 0)),
                  pl.BlockSpec(memory_space=pl.ANY)],
        out_specs=pl.BlockSpec(memory_space=pl.ANY),
        out_shape=jax.ShapeDtypeStruct((n_rows, D_MODEL), F32),
        scratch_shapes=[pltpu.SemaphoreType.DMA],
        input_output_aliases={2: 0},
        compiler_params=_params("arbitrary"),
        name="moe_dispatch",
    )(dest, hn, zeros)


def _expert_body(te_ref, used_ref, x_ref, wg_ref, wu_ref, wd_ref, y_ref):
    i = pl.program_id(0)

    @pl.when(i < used_ref[0])
    def _():
        xb = x_ref[...].astype(BF16)
        g = _dot(xb, wg_ref[0])
        u = _dot(xb, wu_ref[0])
        h = (g / (1.0 + jnp.exp(-g)) * u).astype(BF16)
        y_ref[...] = _dot(h, wd_ref[0])

    @pl.when(i >= used_ref[0])
    def _():
        y_ref[...] = jnp.zeros_like(y_ref)


def _experts(xs, tile_expert, n_used, w_gate, w_up, w_down):
    n_rows = xs.shape[0]
    tm = EXPERT_TILE
    grid_spec = pltpu.PrefetchScalarGridSpec(
        num_scalar_prefetch=2,
        grid=(n_rows // tm,),
        in_specs=[pl.BlockSpec((tm, D_MODEL), lambda i, te, nu: (i, 0)),
                  pl.BlockSpec((1, D_MODEL, D_EXPERT), lambda i, te, nu: (te[i], 0, 0)),
                  pl.BlockSpec((1, D_MODEL, D_EXPERT), lambda i, te, nu: (te[i], 0, 0)),
                  pl.BlockSpec((1, D_EXPERT, D_MODEL), lambda i, te, nu: (te[i], 0, 0))],
        out_specs=pl.BlockSpec((tm, D_MODEL), lambda i, te, nu: (i, 0)),
    )
    return pl.pallas_call(
        _expert_body,
        grid_spec=grid_spec,
        out_shape=jax.ShapeDtypeStruct((n_rows, D_MODEL), F32),
        compiler_params=_params("arbitrary"),
        name="moe_experts",
    )(tile_expert, n_used, xs, w_gate, w_up, w_down)


def _combine_body(dest_ref, x_ref, meta_ref, ys_ref, *rest, final):
    if final:
        g_ref, o_ref, ya_ref, yb_ref, sem = rest
    else:
        o_ref, ya_ref, yb_ref, sem = rest
    tm = x_ref.shape[0]

    def issue(k0, carry):
        for u in range(MOVE_UNROLL):
            k = k0 * MOVE_UNROLL + u
            _row_copy(ys_ref, dest_ref[0, 0, k], ya_ref, k, sem).start()
            _row_copy(ys_ref, dest_ref[0, 0, tm + k], yb_ref, k, sem).start()
        return carry

    lax.fori_loop(0, tm // MOVE_UNROLL, issue, 0)
    for buf in (ya_ref, yb_ref):
        pltpu.make_async_copy(ys_ref.at[pl.ds(0, tm), :], buf, sem).wait()
    meta = meta_ref[...]
    x = x_ref[...] + meta[:, 2:3] * ya_ref[...] + meta[:, 3:4] * yb_ref[...]
    o_ref[...] = _rms(x, g_ref[...]) if final else x


def _combine(x1, meta, dest, ys, r0, rows, final_gain=None):
    tm = MOVE_TILE
    first = r0 // tm
    final = final_gain is not None
    tile = lambda i: (first + i, 0)
    in_specs = [pl.BlockSpec((1, 1, 2 * tm), lambda i: (first + i, 0, 0), memory_space=pltpu.SMEM),
                pl.BlockSpec((tm, D_MODEL), tile),
                pl.BlockSpec((tm, LANES), tile),
                pl.BlockSpec(memory_space=pl.ANY)]
    args = [dest, x1, meta, ys]
    if final:
        in_specs.append(pl.BlockSpec((1, D_MODEL), lambda i: (0, 0)))
        args.append(final_gain.reshape(1, D_MODEL))
    return pl.pallas_call(
        functools.partial(_combine_body, final=final),
        grid=(rows // tm,),
        in_specs=in_specs,
        out_specs=pl.BlockSpec((tm, D_MODEL), lambda i: (i, 0)),
        out_shape=jax.ShapeDtypeStruct((rows, D_MODEL), F32),
        scratch_shapes=[pltpu.VMEM((tm, D_MODEL), F32), pltpu.VMEM((tm, D_MODEL), F32),
                        pltpu.SemaphoreType.DMA],
        compiler_params=_params("arbitrary"),
        name="moe_combine_final" if final else "moe_combine",
    )(*args)


def _moe(x1, hn, logits, w_gate, w_up, w_down, final=None):
    t = x1.shape[0]
    tm = MOVE_TILE
    meta, idx, counts = _route(logits)
    counts = counts[0, :N_EXPERTS].astype(I32)
    padded = (counts + EXPERT_TILE - 1) // EXPERT_TILE * EXPERT_TILE
    ends = jnp.cumsum(padded)
    offsets = ends - padded
    n_rows = 2 * t + N_EXPERTS * EXPERT_TILE
    n_tiles = n_rows // EXPERT_TILE
    n_used = (ends[-1] // EXPERT_TILE).astype(I32)
    tile_start = jnp.minimum(jnp.arange(n_tiles, dtype=I32), n_used - 1) * EXPERT_TILE
    tile_expert = jnp.sum((ends[None, :] <= tile_start[:, None]).astype(I32), axis=1)
    experts = jnp.arange(N_EXPERTS, dtype=I32)[None, :]

    def position(e_row, rank_row):
        e = idx[:, e_row, :].reshape(t).astype(I32)[:, None]
        rank = idx[:, rank_row, :].reshape(t).astype(I32)
        return jnp.sum(jnp.where(e == experts, offsets[None, :], 0), axis=1) + rank

    dest1, dest2 = position(0, 4), position(1, 5)

    def tiled(rows):
        return jnp.concatenate([dest1.reshape(t // rows, 1, rows), dest2.reshape(t // rows, 1, rows)], axis=-1)

    dest = tiled(tm)
    xs = _dispatch(hn, tiled(DISPATCH_TILE), n_rows)
    ys = _experts(xs, tile_expert, n_used.reshape(1), w_gate, w_up, w_down)
    if final is None:
        return _combine(x1, meta, dest, ys, 0, t)
    gain, ranges = final
    return [_combine(x1, meta, dest, ys, r0, rows, gain) for r0, rows in ranges]


def kernel(x_prompt, x_sample, norm_mix, norm_ffn, norm_final, w_in_even, q_gain, k_gain, rpb, w_out_even,
           w_in_odd, lam_q1, lam_k1, lam_q2, lam_k2, subln_gain, w_out_odd,
           w_rg, b_rg, w_re, b_re, w_gate, w_up, w_down):
    bp, lp, d = x_prompt.shape
    bs, ls, _ = x_sample.shape
    assert d == D_MODEL and (bp * lp) % ls == 0 and lp % MOVE_TILE == 0 and ls % MOVE_TILE == 0
    assert (bp * lp + bs * ls) % DISPATCH_TILE == 0
    tp, ts = bp * lp, bs * ls
    t = tp + ts
    seqs = ((0, bp, lp), (tp, bs, ls))
    x = [x_prompt.reshape(tp, d), x_sample.reshape(ts, d)]
    max_len = max(lp, ls)
    depth = norm_mix.shape[0]

    def router(i):
        w = jnp.zeros((D_MODEL, LANES), F32)
        w = w.at[:, :N_GROUPS].set(w_rg[i]).at[:, N_GROUPS:N_GROUPS + N_EXPERTS].set(w_re[i])
        b = jnp.zeros((1, LANES), F32)
        b = b.at[0, :N_GROUPS].set(b_rg[i]).at[0, N_GROUPS:N_GROUPS + N_EXPERTS].set(b_re[i])
        w_hi = w.astype(BF16)
        w_lo = (w - w_hi.astype(F32)).astype(BF16)
        return jnp.concatenate([w_hi, w_lo], axis=1), b

    for i in range(depth):
        j = i // 2
        if i % 2 == 0:
            head_gain = jnp.stack([jnp.tile(q_gain[j], 2), jnp.tile(k_gain[j], 2)])
            a, b, avt = _inproj(x, norm_mix[i], w_in_even[j].astype(BF16), _axial_tables(max_len), seqs, head_gain)
            bias = _natten_bias(rpb[j])
            pieces = [[_gqa(a, avt, seq), _natten(b, bias, seq)] for seq in seqs]
            w_out = w_out_even[j]
        else:
            c, cvt = _inproj(x, norm_mix[i], w_in_odd[j].astype(BF16), _partial_tables(max_len), seqs)
            lam_init = 0.8 - 0.6 * math.exp(-0.3 * i)
            lam_rows = jnp.zeros((8, LANES), F32).at[:4, :HEAD_DIM].set(
                jnp.stack([lam_q1[j], lam_k1[j], lam_q2[j], lam_k2[j]]))
            pieces = [[_diff(c, cvt, lam_rows, subln_gain[j].reshape(1, LANES), lam_init, seq)] for seq in seqs]
            w_out = w_out_odd[j]
        w_router, b_router = router(i)
        x1, hn, logits = _outproj(pieces, w_out.astype(BF16), x, norm_ffn[i], w_router, b_router)
        final = (norm_final, ((0, tp), (tp, ts))) if i == depth - 1 else None
        x = _moe(x1, hn, logits, w_gate[i].astype(BF16), w_up[i].astype(BF16), w_down[i].astype(BF16), final)
        x = x if final else [x]
    y_prompt, y_sample = x
    return (y_prompt.reshape(bp, lp, d), y_sample.reshape(bs, ls, d))
```

```python
import functools
import math

import numpy as np
import jax
import jax.numpy as jnp
from jax import lax
from jax.experimental import pallas as pl
from jax.experimental.pallas import tpu as pltpu

F32 = jnp.float32
BF16 = jnp.bfloat16
I32 = jnp.int32

D_MODEL = 1024
HEAD_DIM = 64
GRID_W = 64
EPS = 1e-6
LANES = 128
MXU_W = 256
A_Q_W, A_KV_W, B_W = 512, 128, 512
EVEN_IN = A_Q_W + 2 * A_KV_W + 3 * B_W
ODD_IN = 3 * D_MODEL
AXIAL_THETA = 10000.0
ROPE_THETA = 500000.0
ROPE_DIMS = HEAD_DIM // 4
NA_ROWS, NA_COLS = 8, 16
NA_QROWS = 4
NA_KROWS = NA_QROWS + NA_ROWS
N_GROUPS, EXPERTS_PER_GROUP = 4, 8
N_EXPERTS = N_GROUPS * EXPERTS_PER_GROUP
D_EXPERT = 512
LOG2E = math.log2(math.e)
SCALE = HEAD_DIM ** -0.5 * LOG2E
NEG = -1e30

VMEM_LIMIT_BYTES = 56 * 1024 * 1024
TOK_TILE = 512
ROUTE_TILE = 512
META_ROWS = 8
DISPATCH_TILE = 2048
MOVE_TILE = 1024
MOVE_UNROLL = 16
EXPERT_TILE = 512
ATT_Q = 256
DIFF_HEADS = 4
KEY_CHUNK = 4096
SCORES_AHEAD = 4

def _params(*sem):
    return pltpu.CompilerParams(dimension_semantics=sem, vmem_limit_bytes=VMEM_LIMIT_BYTES)


def _lane_iota(shape=(1, LANES)):
    return lax.broadcasted_iota(I32, shape, len(shape) - 1)


def _rms(x, gain):
    return x * lax.rsqrt(jnp.mean(x * x, axis=-1, keepdims=True) + EPS) * gain


def _dot(a, b):
    return jnp.dot(a, b, preferred_element_type=F32)


def _dot_nt(a, b):
    return lax.dot_general(a, b, (((1,), (1,)), ((), ())), preferred_element_type=F32)


def _rope(y, tab_ref, shift):
    return (y * tab_ref[0] + pltpu.roll(y, LANES - shift, 1) * tab_ref[1]
            + pltpu.roll(y, shift, 1) * tab_ref[2])


def _rope_tables(angles, n):
    cos = jnp.cos(angles)
    sin = jnp.sin(angles)
    low = jnp.asarray((np.arange(HEAD_DIM) % n) < n // 2)
    tab = jnp.stack([cos, jnp.where(low, -sin, 0.0), jnp.where(low, 0.0, sin)])
    return jnp.concatenate([tab, tab], axis=-1).astype(F32)


def _axial_tables(max_len):
    pos = jnp.arange(max_len)
    row = (pos // GRID_W).astype(F32)
    col = (pos % GRID_W).astype(F32)
    half = HEAD_DIM // 2
    inv = AXIAL_THETA ** (-jnp.arange(0, half, 2, dtype=F32) / half)
    inv2 = jnp.concatenate([inv, inv])
    ang = jnp.concatenate([row[:, None] * inv2[None], col[:, None] * inv2[None]], axis=-1)
    return _rope_tables(ang, half)


def _partial_tables(max_len):
    pos = jnp.arange(max_len).astype(F32)
    inv = ROPE_THETA ** (-jnp.arange(0, ROPE_DIMS, 2, dtype=F32) / ROPE_DIMS)
    inv2 = jnp.concatenate([inv, inv])
    ang = pos[:, None] * inv2[None]
    tab = _rope_tables(jnp.concatenate([ang, jnp.zeros((max_len, HEAD_DIM - ROPE_DIMS), F32)], -1), ROPE_DIMS)
    keep = jnp.asarray(np.tile(np.arange(HEAD_DIM) < ROPE_DIMS, 2))
    return jnp.stack([jnp.where(keep, tab[0], 1.0), jnp.where(keep, tab[1], 0.0), jnp.where(keep, tab[2], 0.0)])


def _natten_bias(rpb):
    c, kc = np.arange(GRID_W)[:, None], np.arange(GRID_W)[None, :]
    cs = np.clip(c - NA_COLS // 2, 0, GRID_W - NA_COLS)
    col_ok = (kc >= cs) & (kc < cs + NA_COLS)
    col_pick = (kc - c + NA_COLS - 1)[None] == np.arange(2 * NA_COLS - 1)[:, None, None]
    by_col = jnp.einsum("hdm,mcq->hdcq", rpb.astype(F32), jnp.asarray(col_pick, F32),
                        precision=lax.Precision.HIGHEST)
    rl, ki = np.arange(NA_QROWS)[:, None], np.arange(NA_KROWS)[None, :]
    out = []
    for delta, first in ((0, 0 * rl), (NA_QROWS, rl), (2 * NA_QROWS, 0 * rl + NA_QROWS)):
        row_ok = (ki >= first) & (ki < first + NA_ROWS)
        row_pick = ((ki - delta - rl + NA_ROWS - 1)[None] == np.arange(2 * NA_ROWS - 1)[:, None, None]) & row_ok
        b = jnp.einsum("drk,hdcq->hrckq", jnp.asarray(row_pick, F32), by_col, precision=lax.Precision.HIGHEST)
        ok = row_ok[:, None, :, None] & col_ok[None, :, None, :]
        b = jnp.where(jnp.asarray(ok)[None], b * LOG2E, NEG)
        out.append(b.reshape(rpb.shape[0], NA_QROWS * GRID_W, NA_KROWS * GRID_W))
    return jnp.stack(out)


def _rows_specs(parts, tm):
    if len(parts) == 1:
        return [pl.BlockSpec((tm, D_MODEL), lambda i: (i, 0))], 0
    first = parts[0].shape[0] // tm
    return [pl.BlockSpec((tm, D_MODEL), lambda i: (jnp.minimum(i, first - 1), 0)),
            pl.BlockSpec((tm, D_MODEL), lambda i: (jnp.maximum(i - first, 0), 0))], first


def _rows_value(refs, first_tiles):
    if len(refs) == 1:
        return refs[0][...]
    return jnp.where(pl.program_id(0) < first_tiles, refs[0][...], refs[1][...])


def _inproj_even_body(*refs, n_x, first_tiles):
    g_ref, w_ref, tab_ref, hg_ref, oa_ref, ob_ref, avt_ref = refs[n_x:]
    hn = _rms(_rows_value(refs[:n_x], first_tiles), g_ref[...]).astype(BF16)
    low = _lane_iota() < HEAD_DIM

    def head_norm_rope(y, gain, scale):
        ss = y * y
        s_lo = jnp.sum(jnp.where(low, ss, 0.0), axis=-1, keepdims=True)
        s_hi = jnp.sum(jnp.where(low, 0.0, ss), axis=-1, keepdims=True)
        y = y * lax.rsqrt(jnp.where(low, s_lo, s_hi) * (1.0 / HEAD_DIM) + EPS) * gain
        return _rope(y, tab_ref, HEAD_DIM // 4) * scale

    for c in range(0, A_Q_W, MXU_W):
        y = _dot(hn, w_ref[:, c:c + MXU_W])
        for h in range(0, MXU_W, LANES):
            oa_ref[:, c + h:c + h + LANES] = head_norm_rope(y[:, h:h + LANES], hg_ref[0:1, :], SCALE).astype(BF16)
    y = _dot(hn, w_ref[:, A_Q_W:A_Q_W + MXU_W])
    oa_ref[:, A_Q_W:A_Q_W + LANES] = head_norm_rope(y[:, :LANES], hg_ref[1:2, :], 1.0).astype(BF16)
    avt_ref[...] = y[:, LANES:].T.astype(BF16)
    c0 = A_Q_W + 2 * A_KV_W
    for c in range(0, 3 * B_W, MXU_W):
        y = _dot(hn, w_ref[:, c0 + c:c0 + c + MXU_W])
        ob_ref[:, c:c + MXU_W] = (y * SCALE if c < B_W else y).astype(BF16)


def _inproj_odd_body(*refs, n_x, first_tiles):
    g_ref, w_ref, tab_ref, oc_ref, cvt_ref = refs[n_x:]
    hn = _rms(_rows_value(refs[:n_x], first_tiles), g_ref[...]).astype(BF16)
    for c in range(0, 2 * D_MODEL, MXU_W):
        y = _dot(hn, w_ref[:, c:c + MXU_W])
        for h in range(0, MXU_W, LANES):
            z = _rope(y[:, h:h + LANES], tab_ref, ROPE_DIMS // 2)
            oc_ref[:, c + h:c + h + LANES] = (z * SCALE if c < D_MODEL else z).astype(BF16)
    for c in range(0, D_MODEL, MXU_W):
        cvt_ref[c:c + MXU_W, :] = _dot(hn, w_ref[:, 2 * D_MODEL + c:2 * D_MODEL + c + MXU_W]).T.astype(BF16)


def _tab_index(seqs, tm):
    (r0, _, l0), (r1, _, l1) = seqs

    def index(i):
        return (0, jnp.where(i < r1 // tm, (i - r0 // tm) % (l0 // tm), (i - r1 // tm) % (l1 // tm)), 0)
    return index


def _inproj(x_parts, gain, w, tab, seqs, head_gain=None):
    t = sum(p.shape[0] for p in x_parts)
    tm = TOK_TILE
    even = head_gain is not None
    x_specs, first_tiles = _rows_specs(x_parts, tm)
    in_specs = x_specs + [pl.BlockSpec((1, D_MODEL), lambda i: (0, 0)),
                          pl.BlockSpec(w.shape, lambda i: (0, 0)),
                          pl.BlockSpec((3, tm, LANES), _tab_index(seqs, tm))]
    args = list(x_parts) + [gain.reshape(1, D_MODEL), w, tab]
    if even:
        in_specs.append(pl.BlockSpec((2, LANES), lambda i: (0, 0)))
        args.append(head_gain)
        widths, t_width = (A_Q_W + A_KV_W, 3 * B_W), A_KV_W
        body = _inproj_even_body
    else:
        widths, t_width = (2 * D_MODEL,), D_MODEL
        body = _inproj_odd_body
    return pl.pallas_call(
        functools.partial(body, n_x=len(x_parts), first_tiles=first_tiles),
        grid=(t // tm,),
        in_specs=in_specs,
        out_specs=[pl.BlockSpec((tm, n), lambda i: (i, 0)) for n in widths]
        + [pl.BlockSpec((t_width, tm), lambda i: (0, i))],
        out_shape=[jax.ShapeDtypeStruct((t, n), BF16) for n in widths]
        + [jax.ShapeDtypeStruct((t_width, t), BF16)],
        compiler_params=_params("parallel"),
        name="inproj_even" if even else "inproj_odd",
    )(*args)


def _outproj_body(*refs, n_pieces, n_x, first_tiles):
    o0, o1 = refs[:n_pieces], refs[n_pieces:2 * n_pieces]
    x_refs = refs[2 * n_pieces:2 * n_pieces + n_x]
    w_ref, g_ref, wr_ref, br_ref, x1_ref, hn_ref, lg_ref = refs[2 * n_pieces + n_x:]
    in_first = pl.program_id(0) < first_tiles
    x1 = _rows_value(x_refs, first_tiles)
    c = 0
    for a_ref, b_ref in zip(o0, o1):
        n = a_ref.shape[1]
        x1 = x1 + _dot(jnp.where(in_first, a_ref[...], b_ref[...]), w_ref[c:c + n, :])
        c += n
    x1_ref[...] = x1
    hn = _rms(x1, g_ref[...])
    hn_ref[...] = hn
    hi = hn.astype(BF16)
    lo = (hn - hi.astype(F32)).astype(BF16)
    both = _dot(hi, wr_ref[...])
    lg_ref[...] = both[:, :LANES] + both[:, LANES:] + _dot(lo, wr_ref[:, :LANES]) + br_ref[...]


def _outproj(pieces, w, x_parts, gain, w_router, b_router):
    t = sum(p.shape[0] for p in x_parts)
    tm = TOK_TILE
    row = lambda i: (i, 0)
    fixed = lambda i: (0, 0)
    first_tiles = pieces[0][0].shape[0] // tm
    piece_specs = (
        [pl.BlockSpec((tm, p.shape[1]), lambda i: (jnp.minimum(i, first_tiles - 1), 0)) for p in pieces[0]]
        + [pl.BlockSpec((tm, p.shape[1]), lambda i: (jnp.maximum(i - first_tiles, 0), 0)) for p in pieces[1]])
    x_specs, x_first = _rows_specs(x_parts, tm)
    assert len(x_parts) == 1 or x_first == first_tiles
    return pl.pallas_call(
        functools.partial(_outproj_body, n_pieces=len(pieces[0]), n_x=len(x_parts), first_tiles=first_tiles),
        grid=(t // tm,),
        in_specs=piece_specs + x_specs + [
            pl.BlockSpec((D_MODEL, D_MODEL), fixed), pl.BlockSpec((1, D_MODEL), fixed),
            pl.BlockSpec((D_MODEL, 2 * LANES), fixed), pl.BlockSpec((1, LANES), fixed)],
        out_specs=[pl.BlockSpec((tm, D_MODEL), row), pl.BlockSpec((tm, D_MODEL), row),
                   pl.BlockSpec((tm, LANES), row)],
        out_shape=[jax.ShapeDtypeStruct((t, D_MODEL), F32), jax.ShapeDtypeStruct((t, D_MODEL), F32),
                   jax.ShapeDtypeStruct((t, LANES), F32)],
        compiler_params=_params("parallel"),
        name="outproj_router",
    )(*pieces[0], *pieces[1], *x_parts, w, gain.reshape(1, D_MODEL), w_router, b_router)


def _softmax_pv(s, v):
    m = jnp.max(s, axis=-1, keepdims=True)
    p = jnp.exp2(s - m)
    l = jnp.sum(p, axis=-1, keepdims=True)
    return _dot(p.astype(BF16), v) / l


def _attend_t(chains, n_keys):
    tq = chains[0][1].shape[0]
    state = [(jnp.full((1, tq), NEG, F32), jnp.zeros((1, tq), F32), None) for _ in chains]
    chunk = min(KEY_CHUNK, n_keys)
    units = [(n, slice(t * chunk, (t + 1) * chunk)) for t in range(n_keys // chunk) for n in range(len(chains))]

    def scores(unit):
        n, rows = unit
        k_ref, q, _ = chains[n]
        return _dot_nt(k_ref[rows, :], q)

    queue = [scores(unit) for unit in units[:SCORES_AHEAD]]
    for u, (n, rows) in enumerate(units):
        s = queue.pop(0)
        if u + SCORES_AHEAD < len(units):
            queue.append(scores(units[u + SCORES_AHEAD]))
        m, l, acc = state[n]
        m_new = jnp.maximum(m, jnp.max(s, axis=0, keepdims=True))
        alpha = jnp.exp2(m - m_new)
        p = jnp.exp2(s - m_new)
        l = alpha * l + jnp.sum(p, axis=0, keepdims=True)
        pv = _dot(chains[n][2][:, rows], p.astype(BF16))
        state[n] = (m_new, l, pv if acc is None else alpha * acc + pv)
    return [acc / l for _, l, acc in state]


def _gqa_body(q_ref, k_ref, vt_ref, o_ref):
    lane = _lane_iota()
    chains = []
    for j in range(A_Q_W // LANES):
        g = j // 2
        keep = (lane >= g * HEAD_DIM) & (lane < (g + 1) * HEAD_DIM)
        q2 = q_ref[:, j * LANES:(j + 1) * LANES].astype(F32)
        for par in range(2):
            qh = q2 if par == g else pltpu.roll(q2, HEAD_DIM, 1)
            qh = jnp.where(keep, qh, 0.0).astype(BF16)
            chains.append((k_ref, qh, vt_ref.at[g * HEAD_DIM:(g + 1) * HEAD_DIM, :]))
    outs = _attend_t(chains, k_ref.shape[0])
    for j in range(A_Q_W // LANES):
        o_ref[:, j * LANES:(j + 1) * LANES] = jnp.concatenate(outs[2 * j:2 * j + 2], axis=0).T.astype(BF16)


def _natten_body(q_ref, k_ref, v_ref, bias_ref, o_ref, *, rows):
    rb = pl.program_id(1)
    start = pl.multiple_of(jnp.clip(rb * NA_QROWS - NA_ROWS // 2, 0, rows - NA_KROWS) * GRID_W, GRID_W)
    lane = _lane_iota()
    nk = NA_KROWS * GRID_W
    heads = []
    for j in range(B_W // LANES):
        q2 = q_ref[:, j * LANES:(j + 1) * LANES].astype(F32)
        for keep in (lane < HEAD_DIM, lane >= HEAD_DIM):
            heads.append((jnp.where(keep, q2, 0.0).astype(BF16), j))

    def scores(h):
        qh, j = heads[h]
        return _dot_nt(qh, k_ref[pl.ds(start, nk), j * LANES:(j + 1) * LANES]) + bias_ref[0, h]

    res = []
    s_next = scores(0)
    for h, (_, j) in enumerate(heads):
        s = s_next
        if h + 1 < len(heads):
            s_next = scores(h + 1)
        res.append(_softmax_pv(s, v_ref[pl.ds(start, nk), j * LANES:(j + 1) * LANES]))
    for j in range(B_W // LANES):
        o_ref[:, j * LANES:(j + 1) * LANES] = jnp.where(lane < HEAD_DIM, res[2 * j], res[2 * j + 1]).astype(BF16)


def _diff_body(lam_ref, sg_ref, q_ref, k_ref, vt_ref, o_ref, *, lam_init):
    lp = lam_ref[...]
    lam = (jnp.exp(jnp.sum(lp[0:1] * lp[1:2], axis=-1, keepdims=True))
           - jnp.exp(jnp.sum(lp[2:3] * lp[3:4], axis=-1, keepdims=True)) + lam_init)
    lane = _lane_iota()
    chains = []
    for h in range(DIFF_HEADS):
        c = h * LANES
        q = q_ref[:, c:c + LANES].astype(F32)
        for qm in (jnp.where(lane < HEAD_DIM, q, 0.0), jnp.where(lane < HEAD_DIM, 0.0, q)):
            chains.append((k_ref.at[:, c:c + LANES], qm.astype(BF16), vt_ref.at[c:c + LANES, :]))
    outs = _attend_t(chains, k_ref.shape[0])
    for h in range(DIFF_HEADS):
        o = outs[2 * h] - lam * outs[2 * h + 1]
        o = o * lax.rsqrt(jnp.mean(o * o, axis=0, keepdims=True) + EPS) * (1.0 - lam_init)
        o_ref[:, h * LANES:(h + 1) * LANES] = (o.T * sg_ref[...]).astype(BF16)


def _gqa(a, avt, seq):
    r0, nb, L = seq
    tq = ATT_Q
    nq = L // tq
    return pl.pallas_call(
        _gqa_body, grid=(nb, nq),
        in_specs=[pl.BlockSpec((tq, A_Q_W), lambda b, i: (r0 // tq + b * nq + i, 0)),
                  pl.BlockSpec((L, LANES), lambda b, i: (r0 // L + b, A_Q_W // LANES)),
                  pl.BlockSpec((avt.shape[0], L), lambda b, i: (0, r0 // L + b))],
        out_specs=pl.BlockSpec((tq, A_Q_W), lambda b, i: (b * nq + i, 0)),
        out_shape=jax.ShapeDtypeStruct((nb * L, A_Q_W), BF16),
        compiler_params=_params("parallel", "parallel"), name="gqa_axial")(a, a, avt)


def _natten(bq, bias, seq):
    r0, nb, L = seq
    rows = L // GRID_W
    assert rows >= NA_KROWS and rows % NA_QROWS == 0
    tq = NA_QROWS * GRID_W
    nq = L // tq

    def bias_index(b, i):
        return (jnp.where(i == 0, 0, jnp.where(i == nq - 1, 2, 1)), 0, 0, 0)

    return pl.pallas_call(
        functools.partial(_natten_body, rows=rows), grid=(nb, nq),
        in_specs=[pl.BlockSpec((tq, B_W), lambda b, i: (r0 // tq + b * nq + i, 0)),
                  pl.BlockSpec((L, B_W), lambda b, i: (r0 // L + b, 1)),
                  pl.BlockSpec((L, B_W), lambda b, i: (r0 // L + b, 2)),
                  pl.BlockSpec((1,) + bias.shape[1:], bias_index)],
        out_specs=pl.BlockSpec((tq, B_W), lambda b, i: (b * nq + i, 0)),
        out_shape=jax.ShapeDtypeStruct((nb * L, B_W), BF16),
        compiler_params=_params("parallel", "arbitrary"), name="natten")(bq, bq, bq, bias)


def _diff(c, cvt, lam_rows, sub_gain, lam_init, seq):
    r0, nb, L = seq
    tq = ATT_Q
    nq = L // tq
    w = DIFF_HEADS * LANES
    ng = D_MODEL // w
    fixed = lambda b, h, i: (0, 0)
    return pl.pallas_call(
        functools.partial(_diff_body, lam_init=lam_init), grid=(nb, ng, nq),
        in_specs=[pl.BlockSpec((8, LANES), fixed), pl.BlockSpec((1, LANES), fixed),
                  pl.BlockSpec((tq, w), lambda b, h, i: (r0 // tq + b * nq + i, h)),
                  pl.BlockSpec((L, w), lambda b, h, i: (r0 // L + b, ng + h)),
                  pl.BlockSpec((w, L), lambda b, h, i: (h, r0 // L + b))],
        out_specs=pl.BlockSpec((tq, w), lambda b, h, i: (b * nq + i, h)),
        out_shape=jax.ShapeDtypeStruct((nb * L, D_MODEL), BF16),
        compiler_params=_params("parallel", "parallel", "parallel"), name="diff_attn",
    )(lam_rows, sub_gain, c, c, cvt)


def _route_body(lg_ref, meta_ref, idx_ref, cnt_ref, run_ref):
    @pl.when(pl.program_id(0) == 0)
    def _():
        run_ref[...] = jnp.zeros_like(run_ref)

    lg = lg_ref[...]
    tm = lg.shape[0]
    lane_i = _lane_iota(lg.shape)
    lane = lane_i.astype(F32)
    far = float(LANES)
    is_g = lane_i < N_GROUPS
    gl = jnp.where(is_g, lg, NEG)
    gmax = jnp.max(gl, axis=-1, keepdims=True)
    gidx = jnp.min(jnp.where(is_g & (gl == gmax), lane, far), axis=-1, keepdims=True)
    g_w = 1.0 / jnp.sum(jnp.where(is_g, jnp.exp(gl - gmax), 0.0), axis=-1, keepdims=True)
    eid_i = lane_i - N_GROUPS
    eid = eid_i.astype(F32)
    grp = lax.shift_right_arithmetic(eid_i, int(math.log2(EXPERTS_PER_GROUP))).astype(F32)
    in_grp = (eid_i >= 0) & (eid_i < N_EXPERTS) & (grp == gidx)
    el = jnp.where(in_grp, lg, NEG)
    e1 = jnp.max(el, axis=-1, keepdims=True)
    i1 = jnp.min(jnp.where(in_grp & (el == e1), eid, far), axis=-1, keepdims=True)
    rest = in_grp & (eid != i1)
    el2 = jnp.where(rest, lg, NEG)
    e2 = jnp.max(el2, axis=-1, keepdims=True)
    i2 = jnp.min(jnp.where(rest & (el2 == e2), eid, far), axis=-1, keepdims=True)
    t = jnp.exp(e2 - e1)
    w1 = g_w / (1.0 + t)
    w2 = g_w * t / (1.0 + t)
    pick = ((lane == i1) | (lane == i2))
    onehot = jnp.where(pick, 1.0, 0.0)
    r = lax.broadcasted_iota(I32, (tm, tm), 0)
    c = lax.broadcasted_iota(I32, (tm, tm), 1)
    before = jnp.where(c < r, 1.0, 0.0).astype(BF16)
    prefix = _dot(before, onehot.astype(BF16)) + run_ref[...]
    rank1 = jnp.sum(jnp.where(lane == i1, prefix, 0.0), axis=-1, keepdims=True)
    rank2 = jnp.sum(jnp.where(lane == i2, prefix, 0.0), axis=-1, keepdims=True)
    run = run_ref[...] + jnp.sum(onehot, axis=0, keepdims=True)
    run_ref[...] = run
    cnt_ref[...] = run
    meta = jnp.zeros(lg.shape, F32)
    for n, col in enumerate((i1, i2, w1, w2, rank1, rank2)):
        meta = jnp.where(lane_i == n, col, meta)
    meta_ref[...] = meta
    idx_ref[0] = meta.T[:META_ROWS, :]


def _route(logits):
    t = logits.shape[0]
    tm = ROUTE_TILE
    return pl.pallas_call(
        _route_body,
        grid=(t // tm,),
        in_specs=[pl.BlockSpec((tm, LANES), lambda i: (i, 0))],
        out_specs=[pl.BlockSpec((tm, LANES), lambda i: (i, 0)),
                   pl.BlockSpec((1, META_ROWS, tm), lambda i: (i, 0, 0)),
                   pl.BlockSpec((1, LANES), lambda i: (0, 0))],
        out_shape=[jax.ShapeDtypeStruct((t, LANES), F32),
                   jax.ShapeDtypeStruct((t // tm, META_ROWS, tm), F32),
                   jax.ShapeDtypeStruct((1, LANES), F32)],
        scratch_shapes=[pltpu.VMEM((1, LANES), F32)],
        compiler_params=_params("arbitrary"),
        name="route",
    )(logits)


def _row_copy(src, i, dst, j, sem):
    return pltpu.make_async_copy(src.at[pl.ds(i, 1), :], dst.at[pl.ds(j, 1), :], sem)


def _dispatch_body(dest_ref, hn_ref, xs_in_ref, xs_ref, sem):
    del xs_in_ref
    tm = hn_ref.shape[0]

    def issue(k0, carry):
        for u in range(MOVE_UNROLL):
            k = k0 * MOVE_UNROLL + u
            _row_copy(hn_ref, k, xs_ref, dest_ref[0, 0, k], sem).start()
            _row_copy(hn_ref, k, xs_ref, dest_ref[0, 0, tm + k], sem).start()
        return carry

    lax.fori_loop(0, tm // MOVE_UNROLL, issue, 0)
    for _ in range(2):
        pltpu.make_async_copy(hn_ref, xs_ref.at[pl.ds(0, tm), :], sem).wait()


def _dispatch(hn, dest, base):
    t = hn.shape[0]
    tm = DISPATCH_TILE
    n_rows = base.shape[0]
    return pl.pallas_call(
        _dispatch_body,
        grid=(t // tm,),
        in_specs=[pl.BlockSpec((1, 1, 2 * tm), lambda i: (i, 0, 0), memory_space=pltpu.SMEM),
                  pl.BlockSpec((tm, D_MODEL), lambda i: (i, 0)),
                  pl.BlockSpec(memory_space=pl.ANY)],
        out_specs=pl.BlockSpec(memory_space=pl.ANY),
        out_shape=jax.ShapeDtypeStruct((n_rows, D_MODEL), F32),
        scratch_shapes=[pltpu.SemaphoreType.DMA],
        input_output_aliases={2: 0},
        compiler_params=_params("arbitrary"),
        name="moe_dispatch",
    )(dest, hn, base)


def _expert_body(te_ref, used_ref, x_ref, wg_ref, wu_ref, wd_ref, y_ref):
    i = pl.program_id(0)

    @pl.when(i < used_ref[0])
    def _():
        xb = x_ref[...].astype(BF16)
        g = _dot(xb, wg_ref[0])
        u = _dot(xb, wu_ref[0])
        h = (g / (1.0 + jnp.exp(-g)) * u).astype(BF16)
        y_ref[...] = _dot(h, wd_ref[0])

    @pl.when(i >= used_ref[0])
    def _():
        y_ref[...] = jnp.zeros_like(y_ref)


def _experts(xs, tile_expert, n_used, layer, w_gate, w_up, w_down):
    n_rows = xs.shape[0]
    tm = EXPERT_TILE
    expert = lambda i, te, nu: (layer, te[i], 0, 0)
    grid_spec = pltpu.PrefetchScalarGridSpec(
        num_scalar_prefetch=2,
        grid=(n_rows // tm,),
        in_specs=[pl.BlockSpec((tm, D_MODEL), lambda i, te, nu: (i, 0)),
                  pl.BlockSpec((None, 1, D_MODEL, D_EXPERT), expert),
                  pl.BlockSpec((None, 1, D_MODEL, D_EXPERT), expert),
                  pl.BlockSpec((None, 1, D_EXPERT, D_MODEL), expert)],
        out_specs=pl.BlockSpec((tm, D_MODEL), lambda i, te, nu: (i, 0)),
    )
    return pl.pallas_call(
        _expert_body,
        grid_spec=grid_spec,
        out_shape=jax.ShapeDtypeStruct((n_rows, D_MODEL), F32),
        compiler_params=_params("arbitrary"),
        name="moe_experts",
    )(tile_expert, n_used, xs, w_gate, w_up, w_down)


def _combine_body(dest_ref, x_ref, meta_ref, ys_ref, *rest, final):
    if final:
        g_ref, o_ref, ya_ref, yb_ref, sem = rest
    else:
        o_ref, ya_ref, yb_ref, sem = rest
    tm = x_ref.shape[0]

    def issue(k0, carry):
        for u in range(MOVE_UNROLL):
            k = k0 * MOVE_UNROLL + u
            _row_copy(ys_ref, dest_ref[0, 0, k], ya_ref, k, sem).start()
            _row_copy(ys_ref, dest_ref[0, 0, tm + k], yb_ref, k, sem).start()
        return carry

    lax.fori_loop(0, tm // MOVE_UNROLL, issue, 0)
    for buf in (ya_ref, yb_ref):
        pltpu.make_async_copy(ys_ref.at[pl.ds(0, tm), :], buf, sem).wait()
    meta = meta_ref[...]
    x = x_ref[...] + meta[:, 2:3] * ya_ref[...] + meta[:, 3:4] * yb_ref[...]
    o_ref[...] = _rms(x, g_ref[...]) if final else x


def _combine(x1, meta, dest, ys, r0, rows, final_gain=None):
    tm = MOVE_TILE
    first = r0 // tm
    final = final_gain is not None
    tile = lambda i: (first + i, 0)
    in_specs = [pl.BlockSpec((1, 1, 2 * tm), lambda i: (first + i, 0, 0), memory_space=pltpu.SMEM),
                pl.BlockSpec((tm, D_MODEL), tile),
                pl.BlockSpec((tm, LANES), tile),
                pl.BlockSpec(memory_space=pl.ANY)]
    args = [dest, x1, meta, ys]
    if final:
        in_specs.append(pl.BlockSpec((1, D_MODEL), lambda i: (0, 0)))
        args.append(final_gain.reshape(1, D_MODEL))
    return pl.pallas_call(
        functools.partial(_combine_body, final=final),
        grid=(rows // tm,),
        in_specs=in_specs,
        out_specs=pl.BlockSpec((tm, D_MODEL), lambda i: (i, 0)),
        out_shape=jax.ShapeDtypeStruct((rows, D_MODEL), F32),
        scratch_shapes=[pltpu.VMEM((tm, D_MODEL), F32), pltpu.VMEM((tm, D_MODEL), F32),
                        pltpu.SemaphoreType.DMA],
        compiler_params=_params("arbitrary"),
        name="moe_combine_final" if final else "moe_combine",
    )(*args)


def _moe(x1, hn, logits, layer, w_gate, w_up, w_down, spare, final=None):
    t = x1.shape[0]
    tm = MOVE_TILE
    meta, idx, counts = _route(logits)
    counts = counts[0, :N_EXPERTS].astype(I32)
    padded = (counts + EXPERT_TILE - 1) // EXPERT_TILE * EXPERT_TILE
    ends = jnp.cumsum(padded)
    offsets = ends - padded
    n_rows = 2 * t + N_EXPERTS * EXPERT_TILE
    n_tiles = n_rows // EXPERT_TILE
    n_used = (ends[-1] // EXPERT_TILE).astype(I32)
    tile_start = jnp.minimum(jnp.arange(n_tiles, dtype=I32), n_used - 1) * EXPERT_TILE
    tile_expert = jnp.sum((ends[None, :] <= tile_start[:, None]).astype(I32), axis=1)
    experts = jnp.arange(N_EXPERTS, dtype=I32)[None, :]

    def position(e_row, rank_row):
        e = idx[:, e_row, :].reshape(t).astype(I32)[:, None]
        rank = idx[:, rank_row, :].reshape(t).astype(I32)
        return jnp.sum(jnp.where(e == experts, offsets[None, :], 0), axis=1) + rank

    dest1, dest2 = position(0, 4), position(1, 5)

    def tiled(rows):
        return jnp.concatenate([dest1.reshape(t // rows, 1, rows), dest2.reshape(t // rows, 1, rows)], axis=-1)

    dest = tiled(tm)
    base = jnp.zeros((n_rows, D_MODEL), F32) if spare is None else spare
    xs = _dispatch(hn, tiled(DISPATCH_TILE), base)
    ys = _experts(xs, tile_expert, n_used.reshape(1), layer, w_gate, w_up, w_down)
    if final is None:
        return _combine(x1, meta, dest, ys, 0, t), ys
    gain, ranges = final
    return [_combine(x1, meta, dest, ys, r0, rows, gain) for r0, rows in ranges], ys


def kernel(x_prompt, x_sample, norm_mix, norm_ffn, norm_final, w_in_even, q_gain, k_gain, rpb, w_out_even,
           w_in_odd, lam_q1, lam_k1, lam_q2, lam_k2, subln_gain, w_out_odd,
           w_rg, b_rg, w_re, b_re, w_gate, w_up, w_down):
    bp, lp, d = x_prompt.shape
    bs, ls, _ = x_sample.shape
    assert d == D_MODEL and (bp * lp) % ls == 0 and lp % MOVE_TILE == 0 and ls % MOVE_TILE == 0
    assert (bp * lp + bs * ls) % DISPATCH_TILE == 0
    tp, ts = bp * lp, bs * ls
    t = tp + ts
    seqs = ((0, bp, lp), (tp, bs, ls))
    x = [x_prompt.reshape(tp, d), x_sample.reshape(ts, d)]
    max_len = max(lp, ls)
    depth = norm_mix.shape[0]

    def router(i):
        w = jnp.zeros((D_MODEL, LANES), F32)
        w = w.at[:, :N_GROUPS].set(w_rg[i]).at[:, N_GROUPS:N_GROUPS + N_EXPERTS].set(w_re[i])
        b = jnp.zeros((1, LANES), F32)
        b = b.at[0, :N_GROUPS].set(b_rg[i]).at[0, N_GROUPS:N_GROUPS + N_EXPERTS].set(b_re[i])
        w_hi = w.astype(BF16)
        w_lo = (w - w_hi.astype(F32)).astype(BF16)
        return jnp.concatenate([w_hi, w_lo], axis=1), b

    expert_w = (w_gate.astype(BF16), w_up.astype(BF16), w_down.astype(BF16))
    spare = None
    for i in range(depth):
        j = i // 2
        if i % 2 == 0:
            head_gain = jnp.stack([jnp.tile(q_gain[j], 2), jnp.tile(k_gain[j], 2)])
            a, b, avt = _inproj(x, norm_mix[i], w_in_even[j].astype(BF16), _axial_tables(max_len), seqs, head_gain)
            bias = _natten_bias(rpb[j])
            pieces = [[_gqa(a, avt, seq), _natten(b, bias, seq)] for seq in seqs]
            w_out = w_out_even[j]
        else:
            c, cvt = _inproj(x, norm_mix[i], w_in_odd[j].astype(BF16), _partial_tables(max_len), seqs)
            lam_init = 0.8 - 0.6 * math.exp(-0.3 * i)
            lam_rows = jnp.zeros((8, LANES), F32).at[:4, :HEAD_DIM].set(
                jnp.stack([lam_q1[j], lam_k1[j], lam_q2[j], lam_k2[j]]))
            pieces = [[_diff(c, cvt, lam_rows, subln_gain[j].reshape(1, LANES), lam_init, seq)] for seq in seqs]
            w_out = w_out_odd[j]
        w_router, b_router = router(i)
        x1, hn, logits = _outproj(pieces, w_out.astype(BF16), x, norm_ffn[i], w_router, b_router)
        final = (norm_final, ((0, tp), (tp, ts))) if i == depth - 1 else None
        x, spare = _moe(x1, hn, logits, i, *expert_w, spare, final)
        x = x if final else [x]
    y_prompt, y_sample = x
    return (y_prompt.reshape(bp, lp, d), y_sample.reshape(bs, ls, d))
```

```python
import functools
import math

import numpy as np
import jax
import jax.numpy as jnp
from jax import lax
from jax.experimental import pallas as pl
from jax.experimental.pallas import tpu as pltpu

F32 = jnp.float32
BF16 = jnp.bfloat16
I32 = jnp.int32

D_MODEL = 1024
HEAD_DIM = 64
GRID_W = 64
EPS = 1e-6
LANES = 128
MXU_W = 256
A_Q_W, A_KV_W, B_W = 512, 128, 512
AXIAL_THETA = 10000.0
ROPE_THETA = 500000.0
ROPE_DIMS = HEAD_DIM // 4
NA_ROWS, NA_COLS = 8, 16
NA_QROWS = 4
NA_KROWS = NA_QROWS + NA_ROWS
N_GROUPS, EXPERTS_PER_GROUP = 4, 8
N_EXPERTS = N_GROUPS * EXPERTS_PER_GROUP
D_EXPERT = 512
LOG2E = math.log2(math.e)
SCALE = HEAD_DIM ** -0.5 * LOG2E
NEG = -1e30

VMEM_LIMIT_BYTES = 56 * 1024 * 1024
TOK_TILE = 512
ROUTE_TILE = 512
META_ROWS = 8
DISPATCH_TILE = 2048
MOVE_TILE = 1024
MOVE_UNROLL = 16
EXPERT_TILE = 512
ATT_Q = 256
DIFF_HEADS = 4
KEY_CHUNK = 4096
GQA_AHEAD = 6
DIFF_AHEAD = 4

def _params(*sem):
    return pltpu.CompilerParams(dimension_semantics=sem, vmem_limit_bytes=VMEM_LIMIT_BYTES)


def _lane_iota(shape=(1, LANES)):
    return lax.broadcasted_iota(I32, shape, len(shape) - 1)


def _rms(x, gain):
    return x * lax.rsqrt(jnp.mean(x * x, axis=-1, keepdims=True) + EPS) * gain


def _dot(a, b):
    return jnp.dot(a, b, preferred_element_type=F32)


def _dot_nt(a, b):
    return lax.dot_general(a, b, (((1,), (1,)), ((), ())), preferred_element_type=F32)


def _rope(y, tab_ref, shift):
    return (y * tab_ref[0] + pltpu.roll(y, LANES - shift, 1) * tab_ref[1]
            + pltpu.roll(y, shift, 1) * tab_ref[2])


def _rope_tables(angles, n):
    cos = jnp.cos(angles)
    sin = jnp.sin(angles)
    low = jnp.asarray((np.arange(HEAD_DIM) % n) < n // 2)
    tab = jnp.stack([cos, jnp.where(low, -sin, 0.0), jnp.where(low, 0.0, sin)])
    return jnp.concatenate([tab, tab], axis=-1).astype(F32)


def _axial_tables(max_len):
    pos = jnp.arange(max_len)
    row = (pos // GRID_W).astype(F32)
    col = (pos % GRID_W).astype(F32)
    half = HEAD_DIM // 2
    inv = AXIAL_THETA ** (-jnp.arange(0, half, 2, dtype=F32) / half)
    inv2 = jnp.concatenate([inv, inv])
    ang = jnp.concatenate([row[:, None] * inv2[None], col[:, None] * inv2[None]], axis=-1)
    return _rope_tables(ang, half)


def _partial_tables(max_len):
    pos = jnp.arange(max_len).astype(F32)
    inv = ROPE_THETA ** (-jnp.arange(0, ROPE_DIMS, 2, dtype=F32) / ROPE_DIMS)
    inv2 = jnp.concatenate([inv, inv])
    ang = pos[:, None] * inv2[None]
    tab = _rope_tables(jnp.concatenate([ang, jnp.zeros((max_len, HEAD_DIM - ROPE_DIMS), F32)], -1), ROPE_DIMS)
    keep = jnp.asarray(np.tile(np.arange(HEAD_DIM) < ROPE_DIMS, 2))
    return jnp.stack([jnp.where(keep, tab[0], 1.0), jnp.where(keep, tab[1], 0.0), jnp.where(keep, tab[2], 0.0)])


def _natten_bias(rpb):
    c, kc = np.arange(GRID_W)[:, None], np.arange(GRID_W)[None, :]
    cs = np.clip(c - NA_COLS // 2, 0, GRID_W - NA_COLS)
    col_ok = (kc >= cs) & (kc < cs + NA_COLS)
    col_pick = (kc - c + NA_COLS - 1)[None] == np.arange(2 * NA_COLS - 1)[:, None, None]
    by_col = jnp.einsum("hdm,mcq->hdcq", rpb.astype(F32), jnp.asarray(col_pick, F32),
                        precision=lax.Precision.HIGHEST)
    rl, ki = np.arange(NA_QROWS)[:, None], np.arange(NA_KROWS)[None, :]
    out = []
    for delta, first in ((0, 0 * rl), (NA_QROWS, rl), (2 * NA_QROWS, 0 * rl + NA_QROWS)):
        row_ok = (ki >= first) & (ki < first + NA_ROWS)
        row_pick = ((ki - delta - rl + NA_ROWS - 1)[None] == np.arange(2 * NA_ROWS - 1)[:, None, None]) & row_ok
        b = jnp.einsum("drk,hdcq->hrckq", jnp.asarray(row_pick, F32), by_col, precision=lax.Precision.HIGHEST)
        ok = row_ok[:, None, :, None] & col_ok[None, :, None, :]
        b = jnp.where(jnp.asarray(ok)[None], b * LOG2E, NEG)
        out.append(b.reshape(rpb.shape[0], NA_QROWS * GRID_W, NA_KROWS * GRID_W))
    return jnp.stack(out)


def _rows_specs(parts, tm):
    if len(parts) == 1:
        return [pl.BlockSpec((tm, D_MODEL), lambda i: (i, 0))], 0
    first = parts[0].shape[0] // tm
    return [pl.BlockSpec((tm, D_MODEL), lambda i: (jnp.minimum(i, first - 1), 0)),
            pl.BlockSpec((tm, D_MODEL), lambda i: (jnp.maximum(i - first, 0), 0))], first


def _rows_value(refs, first_tiles):
    if len(refs) == 1:
        return refs[0][...]
    return jnp.where(pl.program_id(0) < first_tiles, refs[0][...], refs[1][...])


def _inproj_even_body(*refs, n_x, first_tiles):
    g_ref, w_ref, tab_ref, hg_ref, oa_ref, ob_ref, avt_ref = refs[n_x:]
    hn = _rms(_rows_value(refs[:n_x], first_tiles), g_ref[...]).astype(BF16)
    low = _lane_iota() < HEAD_DIM

    def head_norm_rope(y, gain, scale):
        ss = y * y
        s_lo = jnp.sum(jnp.where(low, ss, 0.0), axis=-1, keepdims=True)
        s_hi = jnp.sum(jnp.where(low, 0.0, ss), axis=-1, keepdims=True)
        y = y * lax.rsqrt(jnp.where(low, s_lo, s_hi) * (1.0 / HEAD_DIM) + EPS) * gain
        return _rope(y, tab_ref, HEAD_DIM // 4) * scale

    for c in range(0, A_Q_W, MXU_W):
        y = _dot(hn, w_ref[:, c:c + MXU_W])
        for h in range(0, MXU_W, LANES):
            oa_ref[:, c + h:c + h + LANES] = head_norm_rope(y[:, h:h + LANES], hg_ref[0:1, :], SCALE).astype(BF16)
    y = _dot(hn, w_ref[:, A_Q_W:A_Q_W + MXU_W])
    oa_ref[:, A_Q_W:A_Q_W + LANES] = head_norm_rope(y[:, :LANES], hg_ref[1:2, :], 1.0).astype(BF16)
    avt_ref[...] = y[:, LANES:].T.astype(BF16)
    c0 = A_Q_W + 2 * A_KV_W
    for c in range(0, 3 * B_W, MXU_W):
        y = _dot(hn, w_ref[:, c0 + c:c0 + c + MXU_W])
        ob_ref[:, c:c + MXU_W] = (y * SCALE if c < B_W else y).astype(BF16)


def _inproj_odd_body(*refs, n_x, first_tiles):
    g_ref, w_ref, tab_ref, oc_ref, cvt_ref = refs[n_x:]
    hn = _rms(_rows_value(refs[:n_x], first_tiles), g_ref[...]).astype(BF16)
    for c in range(0, 2 * D_MODEL, MXU_W):
        y = _dot(hn, w_ref[:, c:c + MXU_W])
        for h in range(0, MXU_W, LANES):
            z = _rope(y[:, h:h + LANES], tab_ref, ROPE_DIMS // 2)
            oc_ref[:, c + h:c + h + LANES] = (z * SCALE if c < D_MODEL else z).astype(BF16)
    for c in range(0, D_MODEL, MXU_W):
        cvt_ref[c:c + MXU_W, :] = _dot(hn, w_ref[:, 2 * D_MODEL + c:2 * D_MODEL + c + MXU_W]).T.astype(BF16)


def _tab_index(seqs, tm):
    (r0, _, l0), (r1, _, l1) = seqs

    def index(i):
        return (0, jnp.where(i < r1 // tm, (i - r0 // tm) % (l0 // tm), (i - r1 // tm) % (l1 // tm)), 0)
    return index


def _inproj(x_parts, gain, w, tab, seqs, head_gain=None):
    t = sum(p.shape[0] for p in x_parts)
    tm = TOK_TILE
    even = head_gain is not None
    x_specs, first_tiles = _rows_specs(x_parts, tm)
    in_specs = x_specs + [pl.BlockSpec((1, D_MODEL), lambda i: (0, 0)),
                          pl.BlockSpec(w.shape, lambda i: (0, 0)),
                          pl.BlockSpec((3, tm, LANES), _tab_index(seqs, tm))]
    args = list(x_parts) + [gain.reshape(1, D_MODEL), w, tab]
    if even:
        in_specs.append(pl.BlockSpec((2, LANES), lambda i: (0, 0)))
        args.append(head_gain)
        widths, t_width = (A_Q_W + A_KV_W, 3 * B_W), A_KV_W
        body = _inproj_even_body
    else:
        widths, t_width = (2 * D_MODEL,), D_MODEL
        body = _inproj_odd_body
    return pl.pallas_call(
        functools.partial(body, n_x=len(x_parts), first_tiles=first_tiles),
        grid=(t // tm,),
        in_specs=in_specs,
        out_specs=[pl.BlockSpec((tm, n), lambda i: (i, 0)) for n in widths]
        + [pl.BlockSpec((t_width, tm), lambda i: (0, i))],
        out_shape=[jax.ShapeDtypeStruct((t, n), BF16) for n in widths]
        + [jax.ShapeDtypeStruct((t_width, t), BF16)],
        compiler_params=_params("parallel"),
        name="inproj_even" if even else "inproj_odd",
    )(*args)


def _outproj_body(*refs, n_pieces, n_x, first_tiles):
    o0, o1 = refs[:n_pieces], refs[n_pieces:2 * n_pieces]
    x_refs = refs[2 * n_pieces:2 * n_pieces + n_x]
    w_ref, g_ref, wr_ref, br_ref, x1_ref, hn_ref, lg_ref = refs[2 * n_pieces + n_x:]
    in_first = pl.program_id(0) < first_tiles
    x1 = _rows_value(x_refs, first_tiles)
    c = 0
    for a_ref, b_ref in zip(o0, o1):
        n = a_ref.shape[1]
        x1 = x1 + _dot(jnp.where(in_first, a_ref[...], b_ref[...]), w_ref[c:c + n, :])
        c += n
    x1_ref[...] = x1
    hn = _rms(x1, g_ref[...])
    hn_ref[...] = hn
    hi = hn.astype(BF16)
    lo = (hn - hi.astype(F32)).astype(BF16)
    both = _dot(hi, wr_ref[...])
    lg_ref[...] = both[:, :LANES] + both[:, LANES:] + _dot(lo, wr_ref[:, :LANES]) + br_ref[...]


def _outproj(pieces, w, x_parts, gain, w_router, b_router):
    t = sum(p.shape[0] for p in x_parts)
    tm = TOK_TILE
    row = lambda i: (i, 0)
    fixed = lambda i: (0, 0)
    first_tiles = pieces[0][0].shape[0] // tm
    piece_specs = (
        [pl.BlockSpec((tm, p.shape[1]), lambda i: (jnp.minimum(i, first_tiles - 1), 0)) for p in pieces[0]]
        + [pl.BlockSpec((tm, p.shape[1]), lambda i: (jnp.maximum(i - first_tiles, 0), 0)) for p in pieces[1]])
    x_specs, x_first = _rows_specs(x_parts, tm)
    assert len(x_parts) == 1 or x_first == first_tiles
    return pl.pallas_call(
        functools.partial(_outproj_body, n_pieces=len(pieces[0]), n_x=len(x_parts), first_tiles=first_tiles),
        grid=(t // tm,),
        in_specs=piece_specs + x_specs + [
            pl.BlockSpec((D_MODEL, D_MODEL), fixed), pl.BlockSpec((1, D_MODEL), fixed),
            pl.BlockSpec((D_MODEL, 2 * LANES), fixed), pl.BlockSpec((1, LANES), fixed)],
        out_specs=[pl.BlockSpec((tm, D_MODEL), row), pl.BlockSpec((tm, D_MODEL), row),
                   pl.BlockSpec((tm, LANES), row)],
        out_shape=[jax.ShapeDtypeStruct((t, D_MODEL), F32), jax.ShapeDtypeStruct((t, D_MODEL), F32),
                   jax.ShapeDtypeStruct((t, LANES), F32)],
        compiler_params=_params("parallel"),
        name="outproj_router",
    )(*pieces[0], *pieces[1], *x_parts, w, gain.reshape(1, D_MODEL), w_router, b_router)


def _softmax_pv(s, v):
    m = jnp.max(s, axis=-1, keepdims=True)
    p = jnp.exp2(s - m)
    l = jnp.sum(p, axis=-1, keepdims=True)
    return _dot(p.astype(BF16), v) / l


def _attend_t(chains, n_keys, ahead):
    tq = chains[0][1].shape[0]
    state = [(jnp.full((1, tq), NEG, F32), jnp.zeros((1, tq), F32), None) for _ in chains]
    chunk = min(KEY_CHUNK, n_keys)
    units = [(n, slice(t * chunk, (t + 1) * chunk)) for t in range(n_keys // chunk) for n in range(len(chains))]

    def scores(unit):
        n, rows = unit
        k_ref, q, _ = chains[n]
        return _dot_nt(k_ref[rows, :], q)

    queue = [scores(unit) for unit in units[:ahead]]
    for u, (n, rows) in enumerate(units):
        s = queue.pop(0)
        if u + ahead < len(units):
            queue.append(scores(units[u + ahead]))
        m, l, acc = state[n]
        m_new = jnp.maximum(m, jnp.max(s, axis=0, keepdims=True))
        alpha = jnp.exp2(m - m_new)
        p = jnp.exp2(s - m_new)
        l = alpha * l + jnp.sum(p, axis=0, keepdims=True)
        pv = _dot(chains[n][2][:, rows], p.astype(BF16))
        state[n] = (m_new, l, pv if acc is None else alpha * acc + pv)
    return [acc / l for _, l, acc in state]


def _gqa_body(q_ref, k_ref, vt_ref, o_ref):
    lane = _lane_iota()
    chains = []
    for j in range(A_Q_W // LANES):
        g = j // 2
        keep = (lane >= g * HEAD_DIM) & (lane < (g + 1) * HEAD_DIM)
        q2 = q_ref[:, j * LANES:(j + 1) * LANES].astype(F32)
        for par in range(2):
            qh = q2 if par == g else pltpu.roll(q2, HEAD_DIM, 1)
            qh = jnp.where(keep, qh, 0.0).astype(BF16)
            chains.append((k_ref, qh, vt_ref.at[g * HEAD_DIM:(g + 1) * HEAD_DIM, :]))
    outs = _attend_t(chains, k_ref.shape[0], GQA_AHEAD)
    for j in range(A_Q_W // LANES):
        o_ref[:, j * LANES:(j + 1) * LANES] = jnp.concatenate(outs[2 * j:2 * j + 2], axis=0).T.astype(BF16)


def _natten_body(q_ref, k_ref, v_ref, bias_ref, o_ref, *, rows):
    rb = pl.program_id(1)
    start = pl.multiple_of(jnp.clip(rb * NA_QROWS - NA_ROWS // 2, 0, rows - NA_KROWS) * GRID_W, GRID_W)
    lane = _lane_iota()
    nk = NA_KROWS * GRID_W
    heads = []
    for j in range(B_W // LANES):
        q2 = q_ref[:, j * LANES:(j + 1) * LANES].astype(F32)
        for keep in (lane < HEAD_DIM, lane >= HEAD_DIM):
            heads.append((jnp.where(keep, q2, 0.0).astype(BF16), j))

    def scores(h):
        qh, j = heads[h]
        return _dot_nt(qh, k_ref[pl.ds(start, nk), j * LANES:(j + 1) * LANES]) + bias_ref[0, h]

    res = []
    s_next = scores(0)
    for h, (_, j) in enumerate(heads):
        s = s_next
        if h + 1 < len(heads):
            s_next = scores(h + 1)
        res.append(_softmax_pv(s, v_ref[pl.ds(start, nk), j * LANES:(j + 1) * LANES]))
    for j in range(B_W // LANES):
        o_ref[:, j * LANES:(j + 1) * LANES] = jnp.where(lane < HEAD_DIM, res[2 * j], res[2 * j + 1]).astype(BF16)


def _diff_body(lam_ref, sg_ref, q_ref, k_ref, vt_ref, o_ref, *, lam_init):
    lp = lam_ref[...]
    lam = (jnp.exp(jnp.sum(lp[0:1] * lp[1:2], axis=-1, keepdims=True))
           - jnp.exp(jnp.sum(lp[2:3] * lp[3:4], axis=-1, keepdims=True)) + lam_init)
    lane = _lane_iota()
    chains = []
    for h in range(DIFF_HEADS):
        c = h * LANES
        q = q_ref[:, c:c + LANES].astype(F32)
        for qm in (jnp.where(lane < HEAD_DIM, q, 0.0), jnp.where(lane < HEAD_DIM, 0.0, q)):
            chains.append((k_ref.at[:, c:c + LANES], qm.astype(BF16), vt_ref.at[c:c + LANES, :]))
    outs = _attend_t(chains, k_ref.shape[0], DIFF_AHEAD)
    for h in range(DIFF_HEADS):
        o = outs[2 * h] - lam * outs[2 * h + 1]
        o = o * lax.rsqrt(jnp.mean(o * o, axis=0, keepdims=True) + EPS) * (1.0 - lam_init)
        o_ref[:, h * LANES:(h + 1) * LANES] = (o.T * sg_ref[...]).astype(BF16)


def _gqa(a, avt, seq):
    r0, nb, L = seq
    tq = ATT_Q
    nq = L // tq
    return pl.pallas_call(
        _gqa_body, grid=(nb, nq),
        in_specs=[pl.BlockSpec((tq, A_Q_W), lambda b, i: (r0 // tq + b * nq + i, 0)),
                  pl.BlockSpec((L, LANES), lambda b, i: (r0 // L + b, A_Q_W // LANES)),
                  pl.BlockSpec((avt.shape[0], L), lambda b, i: (0, r0 // L + b))],
        out_specs=pl.BlockSpec((tq, A_Q_W), lambda b, i: (b * nq + i, 0)),
        out_shape=jax.ShapeDtypeStruct((nb * L, A_Q_W), BF16),
        compiler_params=_params("parallel", "parallel"), name="gqa_axial")(a, a, avt)


def _natten(bq, bias, seq):
    r0, nb, L = seq
    rows = L // GRID_W
    assert rows >= NA_KROWS and rows % NA_QROWS == 0
    tq = NA_QROWS * GRID_W
    nq = L // tq

    def bias_index(b, i):
        return (jnp.where(i == 0, 0, jnp.where(i == nq - 1, 2, 1)), 0, 0, 0)

    return pl.pallas_call(
        functools.partial(_natten_body, rows=rows), grid=(nb, nq),
        in_specs=[pl.BlockSpec((tq, B_W), lambda b, i: (r0 // tq + b * nq + i, 0)),
                  pl.BlockSpec((L, B_W), lambda b, i: (r0 // L + b, 1)),
                  pl.BlockSpec((L, B_W), lambda b, i: (r0 // L + b, 2)),
                  pl.BlockSpec((1,) + bias.shape[1:], bias_index)],
        out_specs=pl.BlockSpec((tq, B_W), lambda b, i: (b * nq + i, 0)),
        out_shape=jax.ShapeDtypeStruct((nb * L, B_W), BF16),
        compiler_params=_params("parallel", "arbitrary"), name="natten")(bq, bq, bq, bias)


def _diff(c, cvt, lam_rows, sub_gain, lam_init, seq):
    r0, nb, L = seq
    tq = ATT_Q
    nq = L // tq
    w = DIFF_HEADS * LANES
    ng = D_MODEL // w
    fixed = lambda b, h, i: (0, 0)
    return pl.pallas_call(
        functools.partial(_diff_body, lam_init=lam_init), grid=(nb, ng, nq),
        in_specs=[pl.BlockSpec((8, LANES), fixed), pl.BlockSpec((1, LANES), fixed),
                  pl.BlockSpec((tq, w), lambda b, h, i: (r0 // tq + b * nq + i, h)),
                  pl.BlockSpec((L, w), lambda b, h, i: (r0 // L + b, ng + h)),
                  pl.BlockSpec((w, L), lambda b, h, i: (h, r0 // L + b))],
        out_specs=pl.BlockSpec((tq, w), lambda b, h, i: (b * nq + i, h)),
        out_shape=jax.ShapeDtypeStruct((nb * L, D_MODEL), BF16),
        compiler_params=_params("parallel", "parallel", "parallel"), name="diff_attn",
    )(lam_rows, sub_gain, c, c, cvt)


def _route_body(lg_ref, meta_ref, idx_ref, cnt_ref, run_ref):
    @pl.when(pl.program_id(0) == 0)
    def _():
        run_ref[...] = jnp.zeros_like(run_ref)

    lg = lg_ref[...]
    tm = lg.shape[0]
    lane_i = _lane_iota(lg.shape)
    lane = lane_i.astype(F32)
    far = float(LANES)
    is_g = lane_i < N_GROUPS
    gl = jnp.where(is_g, lg, NEG)
    gmax = jnp.max(gl, axis=-1, keepdims=True)
    gidx = jnp.min(jnp.where(is_g & (gl == gmax), lane, far), axis=-1, keepdims=True)
    g_w = 1.0 / jnp.sum(jnp.where(is_g, jnp.exp(gl - gmax), 0.0), axis=-1, keepdims=True)
    eid_i = lane_i - N_GROUPS
    eid = eid_i.astype(F32)
    grp = lax.shift_right_arithmetic(eid_i, int(math.log2(EXPERTS_PER_GROUP))).astype(F32)
    in_grp = (eid_i >= 0) & (eid_i < N_EXPERTS) & (grp == gidx)
    el = jnp.where(in_grp, lg, NEG)
    e1 = jnp.max(el, axis=-1, keepdims=True)
    i1 = jnp.min(jnp.where(in_grp & (el == e1), eid, far), axis=-1, keepdims=True)
    rest = in_grp & (eid != i1)
    el2 = jnp.where(rest, lg, NEG)
    e2 = jnp.max(el2, axis=-1, keepdims=True)
    i2 = jnp.min(jnp.where(rest & (el2 == e2), eid, far), axis=-1, keepdims=True)
    t = jnp.exp(e2 - e1)
    w1 = g_w / (1.0 + t)
    w2 = g_w * t / (1.0 + t)
    pick = ((lane == i1) | (lane == i2))
    onehot = jnp.where(pick, 1.0, 0.0)
    r = lax.broadcasted_iota(I32, (tm, tm), 0)
    c = lax.broadcasted_iota(I32, (tm, tm), 1)
    before = jnp.where(c < r, 1.0, 0.0).astype(BF16)
    prefix = _dot(before, onehot.astype(BF16)) + run_ref[...]
    rank1 = jnp.sum(jnp.where(lane == i1, prefix, 0.0), axis=-1, keepdims=True)
    rank2 = jnp.sum(jnp.where(lane == i2, prefix, 0.0), axis=-1, keepdims=True)
    run = run_ref[...] + jnp.sum(onehot, axis=0, keepdims=True)
    run_ref[...] = run
    cnt_ref[...] = run
    meta = jnp.zeros(lg.shape, F32)
    for n, col in enumerate((i1, i2, w1, w2, rank1, rank2)):
        meta = jnp.where(lane_i == n, col, meta)
    meta_ref[...] = meta
    idx_ref[0] = meta.T[:META_ROWS, :]


def _route(logits):
    t = logits.shape[0]
    tm = ROUTE_TILE
    return pl.pallas_call(
        _route_body,
        grid=(t // tm,),
        in_specs=[pl.BlockSpec((tm, LANES), lambda i: (i, 0))],
        out_specs=[pl.BlockSpec((tm, LANES), lambda i: (i, 0)),
                   pl.BlockSpec((1, META_ROWS, tm), lambda i: (i, 0, 0)),
                   pl.BlockSpec((1, LANES), lambda i: (0, 0))],
        out_shape=[jax.ShapeDtypeStruct((t, LANES), F32),
                   jax.ShapeDtypeStruct((t // tm, META_ROWS, tm), F32),
                   jax.ShapeDtypeStruct((1, LANES), F32)],
        scratch_shapes=[pltpu.VMEM((1, LANES), F32)],
        compiler_params=_params("arbitrary"),
        name="route",
    )(logits)


def _row_copy(src, i, dst, j, sem):
    return pltpu.make_async_copy(src.at[pl.ds(i, 1), :], dst.at[pl.ds(j, 1), :], sem)


def _dispatch_body(dest_ref, hn_ref, xs_in_ref, xs_ref, sem):
    del xs_in_ref
    tm = hn_ref.shape[0]

    def issue(k0, carry):
        for u in range(MOVE_UNROLL):
            k = k0 * MOVE_UNROLL + u
            _row_copy(hn_ref, k, xs_ref, dest_ref[0, 0, k], sem).start()
            _row_copy(hn_ref, k, xs_ref, dest_ref[0, 0, tm + k], sem).start()
        return carry

    lax.fori_loop(0, tm // MOVE_UNROLL, issue, 0)
    for _ in range(2):
        pltpu.make_async_copy(hn_ref, xs_ref.at[pl.ds(0, tm), :], sem).wait()


def _dispatch(hn, dest, base):
    t = hn.shape[0]
    tm = DISPATCH_TILE
    n_rows = base.shape[0]
    return pl.pallas_call(
        _dispatch_body,
        grid=(t // tm,),
        in_specs=[pl.BlockSpec((1, 1, 2 * tm), lambda i: (i, 0, 0), memory_space=pltpu.SMEM),
                  pl.BlockSpec((tm, D_MODEL), lambda i: (i, 0)),
                  pl.BlockSpec(memory_space=pl.ANY)],
        out_specs=pl.BlockSpec(memory_space=pl.ANY),
        out_shape=jax.ShapeDtypeStruct((n_rows, D_MODEL), F32),
        scratch_shapes=[pltpu.SemaphoreType.DMA],
        input_output_aliases={2: 0},
        compiler_params=_params("arbitrary"),
        name="moe_dispatch",
    )(dest, hn, base)


def _expert_body(te_ref, used_ref, x_ref, wg_ref, wu_ref, wd_ref, y_ref):
    i = pl.program_id(0)

    @pl.when(i < used_ref[0])
    def _():
        xb = x_ref[...].astype(BF16)
        g = _dot(xb, wg_ref[0])
        u = _dot(xb, wu_ref[0])
        h = (g / (1.0 + jnp.exp(-g)) * u).astype(BF16)
        y_ref[...] = _dot(h, wd_ref[0])

    @pl.when(i >= used_ref[0])
    def _():
        y_ref[...] = jnp.zeros_like(y_ref)


def _experts(xs, tile_expert, n_used, layer, w_gate, w_up, w_down):
    n_rows = xs.shape[0]
    tm = EXPERT_TILE
    expert = lambda i, te, nu: (layer, te[i], 0, 0)
    grid_spec = pltpu.PrefetchScalarGridSpec(
        num_scalar_prefetch=2,
        grid=(n_rows // tm,),
        in_specs=[pl.BlockSpec((tm, D_MODEL), lambda i, te, nu: (i, 0)),
                  pl.BlockSpec((None, 1, D_MODEL, D_EXPERT), expert),
                  pl.BlockSpec((None, 1, D_MODEL, D_EXPERT), expert),
                  pl.BlockSpec((None, 1, D_EXPERT, D_MODEL), expert)],
        out_specs=pl.BlockSpec((tm, D_MODEL), lambda i, te, nu: (i, 0)),
    )
    return pl.pallas_call(
        _expert_body,
        grid_spec=grid_spec,
        out_shape=jax.ShapeDtypeStruct((n_rows, D_MODEL), F32),
        compiler_params=_params("arbitrary"),
        name="moe_experts",
    )(tile_expert, n_used, xs, w_gate, w_up, w_down)


def _combine_body(dest_ref, x_ref, meta_ref, ys_ref, *rest, final):
    if final:
        g_ref, o_ref, ya_ref, yb_ref, sem = rest
    else:
        o_ref, ya_ref, yb_ref, sem = rest
    tm = x_ref.shape[0]

    def issue(k0, carry):
        for u in range(MOVE_UNROLL):
            k = k0 * MOVE_UNROLL + u
            _row_copy(ys_ref, dest_ref[0, 0, k], ya_ref, k, sem).start()
            _row_copy(ys_ref, dest_ref[0, 0, tm + k], yb_ref, k, sem).start()
        return carry

    lax.fori_loop(0, tm // MOVE_UNROLL, issue, 0)
    for buf in (ya_ref, yb_ref):
        pltpu.make_async_copy(ys_ref.at[pl.ds(0, tm), :], buf, sem).wait()
    meta = meta_ref[...]
    x = x_ref[...] + meta[:, 2:3] * ya_ref[...] + meta[:, 3:4] * yb_ref[...]
    o_ref[...] = _rms(x, g_ref[...]) if final else x


def _combine(x1, meta, dest, ys, r0, rows, final_gain=None):
    tm = MOVE_TILE
    first = r0 // tm
    final = final_gain is not None
    tile = lambda i: (first + i, 0)
    in_specs = [pl.BlockSpec((1, 1, 2 * tm), lambda i: (first + i, 0, 0), memory_space=pltpu.SMEM),
                pl.BlockSpec((tm, D_MODEL), tile),
                pl.BlockSpec((tm, LANES), tile),
                pl.BlockSpec(memory_space=pl.ANY)]
    args = [dest, x1, meta, ys]
    if final:
        in_specs.append(pl.BlockSpec((1, D_MODEL), lambda i: (0, 0)))
        args.append(final_gain.reshape(1, D_MODEL))
    return pl.pallas_call(
        functools.partial(_combine_body, final=final),
        grid=(rows // tm,),
        in_specs=in_specs,
        out_specs=pl.BlockSpec((tm, D_MODEL), lambda i: (i, 0)),
        out_shape=jax.ShapeDtypeStruct((rows, D_MODEL), F32),
        scratch_shapes=[pltpu.VMEM((tm, D_MODEL), F32), pltpu.VMEM((tm, D_MODEL), F32),
                        pltpu.SemaphoreType.DMA],
        compiler_params=_params("arbitrary"),
        name="moe_combine_final" if final else "moe_combine",
    )(*args)


def _moe(x1, hn, logits, layer, w_gate, w_up, w_down, spare, final=None):
    t = x1.shape[0]
    tm = MOVE_TILE
    meta, idx, counts = _route(logits)
    counts = counts[0, :N_EXPERTS].astype(I32)
    padded = (counts + EXPERT_TILE - 1) // EXPERT_TILE * EXPERT_TILE
    ends = jnp.cumsum(padded)
    offsets = ends - padded
    n_rows = 2 * t + N_EXPERTS * EXPERT_TILE
    n_tiles = n_rows // EXPERT_TILE
    n_used = (ends[-1] // EXPERT_TILE).astype(I32)
    tile_start = jnp.minimum(jnp.arange(n_tiles, dtype=I32), n_used - 1) * EXPERT_TILE
    tile_expert = jnp.sum((ends[None, :] <= tile_start[:, None]).astype(I32), axis=1)
    experts = jnp.arange(N_EXPERTS, dtype=I32)[None, :]

    def position(e_row, rank_row):
        e = idx[:, e_row, :].reshape(t).astype(I32)[:, None]
        rank = idx[:, rank_row, :].reshape(t).astype(I32)
        return jnp.sum(jnp.where(e == experts, offsets[None, :], 0), axis=1) + rank

    dest1, dest2 = position(0, 4), position(1, 5)

    def tiled(rows):
        return jnp.concatenate([dest1.reshape(t // rows, 1, rows), dest2.reshape(t // rows, 1, rows)], axis=-1)

    dest = tiled(tm)
    base = jnp.zeros((n_rows, D_MODEL), F32) if spare is None else spare
    xs = _dispatch(hn, tiled(DISPATCH_TILE), base)
    ys = _experts(xs, tile_expert, n_used.reshape(1), layer, w_gate, w_up, w_down)
    if final is None:
        return _combine(x1, meta, dest, ys, 0, t), ys
    gain, ranges = final
    return [_combine(x1, meta, dest, ys, r0, rows, gain) for r0, rows in ranges], ys


def kernel(x_prompt, x_sample, norm_mix, norm_ffn, norm_final, w_in_even, q_gain, k_gain, rpb, w_out_even,
           w_in_odd, lam_q1, lam_k1, lam_q2, lam_k2, subln_gain, w_out_odd,
           w_rg, b_rg, w_re, b_re, w_gate, w_up, w_down):
    bp, lp, d = x_prompt.shape
    bs, ls, _ = x_sample.shape
    assert d == D_MODEL and (bp * lp) % ls == 0 and lp % MOVE_TILE == 0 and ls % MOVE_TILE == 0
    assert (bp * lp + bs * ls) % DISPATCH_TILE == 0
    tp, ts = bp * lp, bs * ls
    t = tp + ts
    seqs = ((0, bp, lp), (tp, bs, ls))
    x = [x_prompt.reshape(tp, d), x_sample.reshape(ts, d)]
    max_len = max(lp, ls)
    depth = norm_mix.shape[0]

    def router(i):
        w = jnp.zeros((D_MODEL, LANES), F32)
        w = w.at[:, :N_GROUPS].set(w_rg[i]).at[:, N_GROUPS:N_GROUPS + N_EXPERTS].set(w_re[i])
        b = jnp.zeros((1, LANES), F32)
        b = b.at[0, :N_GROUPS].set(b_rg[i]).at[0, N_GROUPS:N_GROUPS + N_EXPERTS].set(b_re[i])
        w_hi = w.astype(BF16)
        w_lo = (w - w_hi.astype(F32)).astype(BF16)
        return jnp.concatenate([w_hi, w_lo], axis=1), b

    expert_w = (w_gate.astype(BF16), w_up.astype(BF16), w_down.astype(BF16))
    spare = None
    for i in range(depth):
        j = i // 2
        if i % 2 == 0:
            head_gain = jnp.stack([jnp.tile(q_gain[j], 2), jnp.tile(k_gain[j], 2)])
            a, b, avt = _inproj(x, norm_mix[i], w_in_even[j].astype(BF16), _axial_tables(max_len), seqs, head_gain)
            bias = _natten_bias(rpb[j])
            pieces = [[_gqa(a, avt, seq), _natten(b, bias, seq)] for seq in seqs]
            w_out = w_out_even[j]
        else:
            c, cvt = _inproj(x, norm_mix[i], w_in_odd[j].astype(BF16), _partial_tables(max_len), seqs)
            lam_init = 0.8 - 0.6 * math.exp(-0.3 * i)
            lam_rows = jnp.zeros((8, LANES), F32).at[:4, :HEAD_DIM].set(
                jnp.stack([lam_q1[j], lam_k1[j], lam_q2[j], lam_k2[j]]))
            pieces = [[_diff(c, cvt, lam_rows, subln_gain[j].reshape(1, LANES), lam_init, seq)] for seq in seqs]
            w_out = w_out_odd[j]
        w_router, b_router = router(i)
        x1, hn, logits = _outproj(pieces, w_out.astype(BF16), x, norm_ffn[i], w_router, b_router)
        final = (norm_final, ((0, tp), (tp, ts))) if i == depth - 1 else None
        x, spare = _moe(x1, hn, logits, i, *expert_w, spare, final)
        x = x if final else [x]
    y_prompt, y_sample = x
    return (y_prompt.reshape(bp, lp, d), y_sample.reshape(bs, ls, d))
```

```python
import functools
import math

import numpy as np
import jax
import jax.numpy as jnp
from jax import lax
from jax.experimental import pallas as pl
from jax.experimental.pallas import tpu as pltpu

F32 = jnp.float32
BF16 = jnp.bfloat16
I32 = jnp.int32

D_MODEL = 1024
HEAD_DIM = 64
GRID_W = 64
EPS = 1e-6
LANES = 128
MXU_W = 256
A_Q_W, A_KV_W, B_W = 512, 128, 512
AXIAL_THETA = 10000.0
ROPE_THETA = 500000.0
ROPE_DIMS = HEAD_DIM // 4
NA_ROWS, NA_COLS = 8, 16
NA_QROWS = 4
NA_KROWS = NA_QROWS + NA_ROWS
N_GROUPS, EXPERTS_PER_GROUP = 4, 8
N_EXPERTS = N_GROUPS * EXPERTS_PER_GROUP
D_EXPERT = 512
LOG2E = math.log2(math.e)
SCALE = HEAD_DIM ** -0.5 * LOG2E
NEG = -1e30

VMEM_LIMIT_BYTES = 56 * 1024 * 1024
TOK_TILE = 512
ROUTE_TILE = 512
META_ROWS = 8
DISPATCH_TILE = 2048
MOVE_TILE = 1024
MOVE_UNROLL = 16
EXPERT_TILE = 512
ATT_Q = 256
DIFF_HEADS = 4
KEY_CHUNK = 4096
GQA_AHEAD = 6
DIFF_AHEAD = 4

def _params(*sem):
    return pltpu.CompilerParams(dimension_semantics=sem, vmem_limit_bytes=VMEM_LIMIT_BYTES)


def _lane_iota(shape=(1, LANES)):
    return lax.broadcasted_iota(I32, shape, len(shape) - 1)


def _rms(x, gain):
    return x * lax.rsqrt(jnp.mean(x * x, axis=-1, keepdims=True) + EPS) * gain


def _dot(a, b):
    return jnp.dot(a, b, preferred_element_type=F32)


def _dot_nt(a, b):
    return lax.dot_general(a, b, (((1,), (1,)), ((), ())), preferred_element_type=F32)


def _rope(y, tab_ref, shift):
    return (y * tab_ref[0] + pltpu.roll(y, LANES - shift, 1) * tab_ref[1]
            + pltpu.roll(y, shift, 1) * tab_ref[2])


def _rope_tables(angles, n):
    cos = jnp.cos(angles)
    sin = jnp.sin(angles)
    low = jnp.asarray((np.arange(HEAD_DIM) % n) < n // 2)
    tab = jnp.stack([cos, jnp.where(low, -sin, 0.0), jnp.where(low, 0.0, sin)])
    return jnp.concatenate([tab, tab], axis=-1).astype(F32)


def _axial_tables(max_len):
    pos = jnp.arange(max_len)
    row = (pos // GRID_W).astype(F32)
    col = (pos % GRID_W).astype(F32)
    half = HEAD_DIM // 2
    inv = AXIAL_THETA ** (-jnp.arange(0, half, 2, dtype=F32) / half)
    inv2 = jnp.concatenate([inv, inv])
    ang = jnp.concatenate([row[:, None] * inv2[None], col[:, None] * inv2[None]], axis=-1)
    return _rope_tables(ang, half)


def _partial_tables(max_len):
    pos = jnp.arange(max_len).astype(F32)
    inv = ROPE_THETA ** (-jnp.arange(0, ROPE_DIMS, 2, dtype=F32) / ROPE_DIMS)
    inv2 = jnp.concatenate([inv, inv])
    ang = pos[:, None] * inv2[None]
    tab = _rope_tables(jnp.concatenate([ang, jnp.zeros((max_len, HEAD_DIM - ROPE_DIMS), F32)], -1), ROPE_DIMS)
    keep = jnp.asarray(np.tile(np.arange(HEAD_DIM) < ROPE_DIMS, 2))
    return jnp.stack([jnp.where(keep, tab[0], 1.0), jnp.where(keep, tab[1], 0.0), jnp.where(keep, tab[2], 0.0)])


def _natten_bias(rpb):
    c, kc = np.arange(GRID_W)[:, None], np.arange(GRID_W)[None, :]
    cs = np.clip(c - NA_COLS // 2, 0, GRID_W - NA_COLS)
    col_ok = (kc >= cs) & (kc < cs + NA_COLS)
    col_pick = (kc - c + NA_COLS - 1)[None] == np.arange(2 * NA_COLS - 1)[:, None, None]
    by_col = jnp.einsum("hdm,mcq->hdcq", rpb.astype(F32), jnp.asarray(col_pick, F32),
                        precision=lax.Precision.HIGHEST)
    rl, ki = np.arange(NA_QROWS)[:, None], np.arange(NA_KROWS)[None, :]
    out = []
    for delta, first in ((0, 0 * rl), (NA_QROWS, rl), (2 * NA_QROWS, 0 * rl + NA_QROWS)):
        row_ok = (ki >= first) & (ki < first + NA_ROWS)
        row_pick = ((ki - delta - rl + NA_ROWS - 1)[None] == np.arange(2 * NA_ROWS - 1)[:, None, None]) & row_ok
        b = jnp.einsum("drk,hdcq->hrckq", jnp.asarray(row_pick, F32), by_col, precision=lax.Precision.HIGHEST)
        ok = row_ok[:, None, :, None] & col_ok[None, :, None, :]
        b = jnp.where(jnp.asarray(ok)[None], b * LOG2E, NEG)
        out.append(b.reshape(rpb.shape[0], NA_QROWS * GRID_W, NA_KROWS * GRID_W))
    return jnp.stack(out)


def _rows_specs(parts, tm):
    if len(parts) == 1:
        return [pl.BlockSpec((tm, D_MODEL), lambda i: (i, 0))], 0
    first = parts[0].shape[0] // tm
    return [pl.BlockSpec((tm, D_MODEL), lambda i: (jnp.minimum(i, first - 1), 0)),
            pl.BlockSpec((tm, D_MODEL), lambda i: (jnp.maximum(i - first, 0), 0))], first


def _rows_value(refs, first_tiles):
    if len(refs) == 1:
        return refs[0][...]
    return jnp.where(pl.program_id(0) < first_tiles, refs[0][...], refs[1][...])


def _inproj_even_body(*refs, n_x, first_tiles):
    g_ref, w_ref, tab_ref, hg_ref, oa_ref, ob_ref, avt_ref = refs[n_x:]
    hn = _rms(_rows_value(refs[:n_x], first_tiles), g_ref[...]).astype(BF16)
    low = _lane_iota() < HEAD_DIM

    def head_norm_rope(y, gain, scale):
        ss = y * y
        s_lo = jnp.sum(jnp.where(low, ss, 0.0), axis=-1, keepdims=True)
        s_hi = jnp.sum(jnp.where(low, 0.0, ss), axis=-1, keepdims=True)
        y = y * lax.rsqrt(jnp.where(low, s_lo, s_hi) * (1.0 / HEAD_DIM) + EPS) * gain
        return _rope(y, tab_ref, HEAD_DIM // 4) * scale

    for c in range(0, A_Q_W, MXU_W):
        y = _dot(hn, w_ref[:, c:c + MXU_W])
        for h in range(0, MXU_W, LANES):
            oa_ref[:, c + h:c + h + LANES] = head_norm_rope(y[:, h:h + LANES], hg_ref[0:1, :], SCALE).astype(BF16)
    y = _dot(hn, w_ref[:, A_Q_W:A_Q_W + MXU_W])
    oa_ref[:, A_Q_W:A_Q_W + LANES] = head_norm_rope(y[:, :LANES], hg_ref[1:2, :], 1.0).astype(BF16)
    avt_ref[...] = y[:, LANES:].T.astype(BF16)
    c0 = A_Q_W + 2 * A_KV_W
    for c in range(0, 3 * B_W, MXU_W):
        y = _dot(hn, w_ref[:, c0 + c:c0 + c + MXU_W])
        ob_ref[:, c:c + MXU_W] = (y * SCALE if c < B_W else y).astype(BF16)


def _inproj_odd_body(*refs, n_x, first_tiles):
    g_ref, w_ref, tab_ref, oc_ref, cvt_ref = refs[n_x:]
    hn = _rms(_rows_value(refs[:n_x], first_tiles), g_ref[...]).astype(BF16)
    for c in range(0, 2 * D_MODEL, MXU_W):
        y = _dot(hn, w_ref[:, c:c + MXU_W])
        for h in range(0, MXU_W, LANES):
            z = _rope(y[:, h:h + LANES], tab_ref, ROPE_DIMS // 2)
            oc_ref[:, c + h:c + h + LANES] = (z * SCALE if c < D_MODEL else z).astype(BF16)
    for c in range(0, D_MODEL, MXU_W):
        cvt_ref[c:c + MXU_W, :] = _dot(hn, w_ref[:, 2 * D_MODEL + c:2 * D_MODEL + c + MXU_W]).T.astype(BF16)


def _tab_index(seqs, tm):
    (r0, _, l0), (r1, _, l1) = seqs

    def index(i):
        return (0, jnp.where(i < r1 // tm, (i - r0 // tm) % (l0 // tm), (i - r1 // tm) % (l1 // tm)), 0)
    return index


def _inproj(x_parts, gain, w, tab, seqs, head_gain=None):
    t = sum(p.shape[0] for p in x_parts)
    tm = TOK_TILE
    even = head_gain is not None
    x_specs, first_tiles = _rows_specs(x_parts, tm)
    in_specs = x_specs + [pl.BlockSpec((1, D_MODEL), lambda i: (0, 0)),
                          pl.BlockSpec(w.shape, lambda i: (0, 0)),
                          pl.BlockSpec((3, tm, LANES), _tab_index(seqs, tm))]
    args = list(x_parts) + [gain.reshape(1, D_MODEL), w, tab]
    if even:
        in_specs.append(pl.BlockSpec((2, LANES), lambda i: (0, 0)))
        args.append(head_gain)
        widths, t_width = (A_Q_W + A_KV_W, 3 * B_W), A_KV_W
        body = _inproj_even_body
    else:
        widths, t_width = (2 * D_MODEL,), D_MODEL
        body = _inproj_odd_body
    return pl.pallas_call(
        functools.partial(body, n_x=len(x_parts), first_tiles=first_tiles),
        grid=(t // tm,),
        in_specs=in_specs,
        out_specs=[pl.BlockSpec((tm, n), lambda i: (i, 0)) for n in widths]
        + [pl.BlockSpec((t_width, tm), lambda i: (0, i))],
        out_shape=[jax.ShapeDtypeStruct((t, n), BF16) for n in widths]
        + [jax.ShapeDtypeStruct((t_width, t), BF16)],
        compiler_params=_params("parallel"),
        name="inproj_even" if even else "inproj_odd",
    )(*args)


def _outproj_body(*refs, n_pieces, n_x, first_tiles):
    o0, o1 = refs[:n_pieces], refs[n_pieces:2 * n_pieces]
    x_refs = refs[2 * n_pieces:2 * n_pieces + n_x]
    w_ref, g_ref, wr_ref, br_ref, x1_ref, hn_ref, lg_ref = refs[2 * n_pieces + n_x:]
    in_first = pl.program_id(0) < first_tiles
    x1 = _rows_value(x_refs, first_tiles)
    c = 0
    for a_ref, b_ref in zip(o0, o1):
        n = a_ref.shape[1]
        x1 = x1 + _dot(jnp.where(in_first, a_ref[...], b_ref[...]), w_ref[c:c + n, :])
        c += n
    x1_ref[...] = x1
    hn = _rms(x1, g_ref[...])
    hn_ref[...] = hn
    hi = hn.astype(BF16)
    lo = (hn - hi.astype(F32)).astype(BF16)
    both = _dot(hi, wr_ref[...])
    lg_ref[...] = both[:, :LANES] + both[:, LANES:] + _dot(lo, wr_ref[:, :LANES]) + br_ref[...]


def _outproj(pieces, w, x_parts, gain, w_router, b_router):
    t = sum(p.shape[0] for p in x_parts)
    tm = TOK_TILE
    row = lambda i: (i, 0)
    fixed = lambda i: (0, 0)
    first_tiles = pieces[0][0].shape[0] // tm
    piece_specs = (
        [pl.BlockSpec((tm, p.shape[1]), lambda i: (jnp.minimum(i, first_tiles - 1), 0)) for p in pieces[0]]
        + [pl.BlockSpec((tm, p.shape[1]), lambda i: (jnp.maximum(i - first_tiles, 0), 0)) for p in pieces[1]])
    x_specs, x_first = _rows_specs(x_parts, tm)
    assert len(x_parts) == 1 or x_first == first_tiles
    return pl.pallas_call(
        functools.partial(_outproj_body, n_pieces=len(pieces[0]), n_x=len(x_parts), first_tiles=first_tiles),
        grid=(t // tm,),
        in_specs=piece_specs + x_specs + [
            pl.BlockSpec((D_MODEL, D_MODEL), fixed), pl.BlockSpec((1, D_MODEL), fixed),
            pl.BlockSpec((D_MODEL, 2 * LANES), fixed), pl.BlockSpec((1, LANES), fixed)],
        out_specs=[pl.BlockSpec((tm, D_MODEL), row), pl.BlockSpec((tm, D_MODEL), row),
                   pl.BlockSpec((tm, LANES), row)],
        out_shape=[jax.ShapeDtypeStruct((t, D_MODEL), F32), jax.ShapeDtypeStruct((t, D_MODEL), F32),
                   jax.ShapeDtypeStruct((t, LANES), F32)],
        compiler_params=_params("parallel"),
        name="outproj_router",
    )(*pieces[0], *pieces[1], *x_parts, w, gain.reshape(1, D_MODEL), w_router, b_router)


def _softmax_pv(s, v):
    m = jnp.max(s, axis=-1, keepdims=True)
    p = jnp.exp2(s - m)
    l = jnp.sum(p, axis=-1, keepdims=True)
    return _dot(p.astype(BF16), v) / l


def _attend_t(chains, n_keys, ahead):
    tq = chains[0][1].shape[0]
    state = [(jnp.full((1, tq), NEG, F32), jnp.zeros((1, tq), F32), None) for _ in chains]
    chunk = min(KEY_CHUNK, n_keys)
    units = [(n, slice(t * chunk, (t + 1) * chunk)) for t in range(n_keys // chunk) for n in range(len(chains))]

    def scores(unit):
        n, rows = unit
        k_ref, q, _ = chains[n]
        return _dot_nt(k_ref[rows, :], q)

    queue = [scores(unit) for unit in units[:ahead]]
    for u, (n, rows) in enumerate(units):
        s = queue.pop(0)
        if u + ahead < len(units):
            queue.append(scores(units[u + ahead]))
        m, l, acc = state[n]
        m_new = jnp.maximum(m, jnp.max(s, axis=0, keepdims=True))
        alpha = jnp.exp2(m - m_new)
        p = jnp.exp2(s - m_new)
        l = alpha * l + jnp.sum(p, axis=0, keepdims=True)
        pv = _dot(chains[n][2][:, rows], p.astype(BF16))
        state[n] = (m_new, l, pv if acc is None else alpha * acc + pv)
    return [acc / l for _, l, acc in state]


def _gqa_body(q_ref, k_ref, vt_ref, o_ref):
    lane = _lane_iota()
    chains = []
    for j in range(A_Q_W // LANES):
        g = j // 2
        keep = (lane >= g * HEAD_DIM) & (lane < (g + 1) * HEAD_DIM)
        q2 = q_ref[:, j * LANES:(j + 1) * LANES].astype(F32)
        for par in range(2):
            qh = q2 if par == g else pltpu.roll(q2, HEAD_DIM, 1)
            qh = jnp.where(keep, qh, 0.0).astype(BF16)
            chains.append((k_ref, qh, vt_ref.at[g * HEAD_DIM:(g + 1) * HEAD_DIM, :]))
    outs = _attend_t(chains, k_ref.shape[0], GQA_AHEAD)
    for j in range(A_Q_W // LANES):
        o_ref[:, j * LANES:(j + 1) * LANES] = jnp.concatenate(outs[2 * j:2 * j + 2], axis=0).T.astype(BF16)


def _natten_body(q_ref, k_ref, v_ref, bias_ref, o_ref, *, rows):
    rb = pl.program_id(1)
    start = pl.multiple_of(jnp.clip(rb * NA_QROWS - NA_ROWS // 2, 0, rows - NA_KROWS) * GRID_W, GRID_W)
    lane = _lane_iota()
    nk = NA_KROWS * GRID_W
    heads = []
    for j in range(B_W // LANES):
        q2 = q_ref[:, j * LANES:(j + 1) * LANES].astype(F32)
        for keep in (lane < HEAD_DIM, lane >= HEAD_DIM):
            heads.append((jnp.where(keep, q2, 0.0).astype(BF16), j))

    def scores(h):
        qh, j = heads[h]
        return _dot_nt(qh, k_ref[pl.ds(start, nk), j * LANES:(j + 1) * LANES]) + bias_ref[0, h]

    res = []
    s_next = scores(0)
    for h, (_, j) in enumerate(heads):
        s = s_next
        if h + 1 < len(heads):
            s_next = scores(h + 1)
        res.append(_softmax_pv(s, v_ref[pl.ds(start, nk), j * LANES:(j + 1) * LANES]))
    for j in range(B_W // LANES):
        o_ref[:, j * LANES:(j + 1) * LANES] = jnp.where(lane < HEAD_DIM, res[2 * j], res[2 * j + 1]).astype(BF16)


def _diff_body(lam_ref, sg_ref, q_ref, k_ref, vt_ref, o_ref, *, lam_init):
    lp = lam_ref[...]
    lam = (jnp.exp(jnp.sum(lp[0:1] * lp[1:2], axis=-1, keepdims=True))
           - jnp.exp(jnp.sum(lp[2:3] * lp[3:4], axis=-1, keepdims=True)) + lam_init)
    lane = _lane_iota()
    chains = []
    for h in range(DIFF_HEADS):
        c = h * LANES
        q = q_ref[:, c:c + LANES].astype(F32)
        for qm in (jnp.where(lane < HEAD_DIM, q, 0.0), jnp.where(lane < HEAD_DIM, 0.0, q)):
            chains.append((k_ref.at[:, c:c + LANES], qm.astype(BF16), vt_ref.at[c:c + LANES, :]))
    outs = _attend_t(chains, k_ref.shape[0], DIFF_AHEAD)
    for h in range(DIFF_HEADS):
        o = outs[2 * h] - lam * outs[2 * h + 1]
        o = o * lax.rsqrt(jnp.mean(o * o, axis=0, keepdims=True) + EPS) * (1.0 - lam_init)
        o_ref[:, h * LANES:(h + 1) * LANES] = (o.T * sg_ref[...]).astype(BF16)


def _gqa(a, avt, seq):
    r0, nb, L = seq
    tq = ATT_Q
    nq = L // tq
    return pl.pallas_call(
        _gqa_body, grid=(nb, nq),
        in_specs=[pl.BlockSpec((tq, A_Q_W), lambda b, i: (r0 // tq + b * nq + i, 0)),
                  pl.BlockSpec((L, LANES), lambda b, i: (r0 // L + b, A_Q_W // LANES)),
                  pl.BlockSpec((avt.shape[0], L), lambda b, i: (0, r0 // L + b))],
        out_specs=pl.BlockSpec((tq, A_Q_W), lambda b, i: (b * nq + i, 0)),
        out_shape=jax.ShapeDtypeStruct((nb * L, A_Q_W), BF16),
        compiler_params=_params("parallel", "parallel"), name="gqa_axial")(a, a, avt)


def _natten(bq, bias, seq):
    r0, nb, L = seq
    rows = L // GRID_W
    assert rows >= NA_KROWS and rows % NA_QROWS == 0
    tq = NA_QROWS * GRID_W
    nq = L // tq

    def bias_index(b, i):
        return (jnp.where(i == 0, 0, jnp.where(i == nq - 1, 2, 1)), 0, 0, 0)

    return pl.pallas_call(
        functools.partial(_natten_body, rows=rows), grid=(nb, nq),
        in_specs=[pl.BlockSpec((tq, B_W), lambda b, i: (r0 // tq + b * nq + i, 0)),
                  pl.BlockSpec((L, B_W), lambda b, i: (r0 // L + b, 1)),
                  pl.BlockSpec((L, B_W), lambda b, i: (r0 // L + b, 2)),
                  pl.BlockSpec((1,) + bias.shape[1:], bias_index)],
        out_specs=pl.BlockSpec((tq, B_W), lambda b, i: (b * nq + i, 0)),
        out_shape=jax.ShapeDtypeStruct((nb * L, B_W), BF16),
        compiler_params=_params("parallel", "arbitrary"), name="natten")(bq, bq, bq, bias)


def _diff(c, cvt, lam_rows, sub_gain, lam_init, seq):
    r0, nb, L = seq
    tq = ATT_Q
    nq = L // tq
    w = DIFF_HEADS * LANES
    ng = D_MODEL // w
    fixed = lambda b, h, i: (0, 0)
    return pl.pallas_call(
        functools.partial(_diff_body, lam_init=lam_init), grid=(nb, ng, nq),
        in_specs=[pl.BlockSpec((8, LANES), fixed), pl.BlockSpec((1, LANES), fixed),
                  pl.BlockSpec((tq, w), lambda b, h, i: (r0 // tq + b * nq + i, h)),
                  pl.BlockSpec((L, w), lambda b, h, i: (r0 // L + b, ng + h)),
                  pl.BlockSpec((w, L), lambda b, h, i: (h, r0 // L + b))],
        out_specs=pl.BlockSpec((tq, w), lambda b, h, i: (b * nq + i, h)),
        out_shape=jax.ShapeDtypeStruct((nb * L, D_MODEL), BF16),
        compiler_params=_params("parallel", "parallel", "parallel"), name="diff_attn",
    )(lam_rows, sub_gain, c, c, cvt)


def _route_body(lg_ref, meta_ref, idx_ref, cnt_ref, run_ref):
    @pl.when(pl.program_id(0) == 0)
    def _():
        run_ref[...] = jnp.zeros_like(run_ref)

    lg = lg_ref[...]
    tm = lg.shape[0]
    lane_i = _lane_iota(lg.shape)
    lane = lane_i.astype(F32)
    far = float(LANES)
    is_g = lane_i < N_GROUPS
    gl = jnp.where(is_g, lg, NEG)
    gmax = jnp.max(gl, axis=-1, keepdims=True)
    gidx = jnp.min(jnp.where(is_g & (gl == gmax), lane, far), axis=-1, keepdims=True)
    g_w = 1.0 / jnp.sum(jnp.where(is_g, jnp.exp(gl - gmax), 0.0), axis=-1, keepdims=True)
    eid_i = lane_i - N_GROUPS
    eid = eid_i.astype(F32)
    grp = lax.shift_right_arithmetic(eid_i, int(math.log2(EXPERTS_PER_GROUP))).astype(F32)
    in_grp = (eid_i >= 0) & (eid_i < N_EXPERTS) & (grp == gidx)
    el = jnp.where(in_grp, lg, NEG)
    e1 = jnp.max(el, axis=-1, keepdims=True)
    i1 = jnp.min(jnp.where(in_grp & (el == e1), eid, far), axis=-1, keepdims=True)
    rest = in_grp & (eid != i1)
    el2 = jnp.where(rest, lg, NEG)
    e2 = jnp.max(el2, axis=-1, keepdims=True)
    i2 = jnp.min(jnp.where(rest & (el2 == e2), eid, far), axis=-1, keepdims=True)
    t = jnp.exp(e2 - e1)
    w1 = g_w / (1.0 + t)
    w2 = g_w * t / (1.0 + t)
    pick = ((lane == i1) | (lane == i2))
    onehot = jnp.where(pick, 1.0, 0.0)
    r = lax.broadcasted_iota(I32, (tm, tm), 0)
    c = lax.broadcasted_iota(I32, (tm, tm), 1)
    before = jnp.where(c < r, 1.0, 0.0).astype(BF16)
    prefix = _dot(before, onehot.astype(BF16)) + run_ref[...]
    rank1 = jnp.sum(jnp.where(lane == i1, prefix, 0.0), axis=-1, keepdims=True)
    rank2 = jnp.sum(jnp.where(lane == i2, prefix, 0.0), axis=-1, keepdims=True)
    run = run_ref[...] + jnp.sum(onehot, axis=0, keepdims=True)
    run_ref[...] = run
    cnt_ref[...] = run
    meta = jnp.zeros(lg.shape, F32)
    for n, col in enumerate((i1, i2, w1, w2, rank1, rank2)):
        meta = jnp.where(lane_i == n, col, meta)
    meta_ref[...] = meta
    idx_ref[0] = meta.T[:META_ROWS, :]


def _route(logits):
    t = logits.shape[0]
    tm = ROUTE_TILE
    return pl.pallas_call(
        _route_body,
        grid=(t // tm,),
        in_specs=[pl.BlockSpec((tm, LANES), lambda i: (i, 0))],
        out_specs=[pl.BlockSpec((tm, LANES), lambda i: (i, 0)),
                   pl.BlockSpec((1, META_ROWS, tm), lambda i: (i, 0, 0)),
                   pl.BlockSpec((1, LANES), lambda i: (0, 0))],
        out_shape=[jax.ShapeDtypeStruct((t, LANES), F32),
                   jax.ShapeDtypeStruct((t // tm, META_ROWS, tm), F32),
                   jax.ShapeDtypeStruct((1, LANES), F32)],
        scratch_shapes=[pltpu.VMEM((1, LANES), F32)],
        compiler_params=_params("arbitrary"),
        name="route",
    )(logits)


def _row_copy(src, i, dst, j, sem):
    return pltpu.make_async_copy(src.at[pl.ds(i, 1), :], dst.at[pl.ds(j, 1), :], sem)


def _dispatch_body(dest_ref, hn_ref, xs_in_ref, xs_ref, sem):
    del xs_in_ref
    tm = hn_ref.shape[0]

    def issue(k0, carry):
        for u in range(MOVE_UNROLL):
            k = k0 * MOVE_UNROLL + u
            _row_copy(hn_ref, k, xs_ref, dest_ref[0, 0, k], sem).start(priority=0)
            _row_copy(hn_ref, k, xs_ref, dest_ref[0, 0, tm + k], sem).start(priority=1)
        return carry

    lax.fori_loop(0, tm // MOVE_UNROLL, issue, 0)
    for _ in range(2):
        pltpu.make_async_copy(hn_ref, xs_ref.at[pl.ds(0, tm), :], sem).wait()


def _dispatch(hn, dest, base):
    t = hn.shape[0]
    tm = DISPATCH_TILE
    n_rows = base.shape[0]
    return pl.pallas_call(
        _dispatch_body,
        grid=(t // tm,),
        in_specs=[pl.BlockSpec((1, 1, 2 * tm), lambda i: (i, 0, 0), memory_space=pltpu.SMEM),
                  pl.BlockSpec((tm, D_MODEL), lambda i: (i, 0)),
                  pl.BlockSpec(memory_space=pl.ANY)],
        out_specs=pl.BlockSpec(memory_space=pl.ANY),
        out_shape=jax.ShapeDtypeStruct((n_rows, D_MODEL), F32),
        scratch_shapes=[pltpu.SemaphoreType.DMA],
        input_output_aliases={2: 0},
        compiler_params=_params("arbitrary"),
        name="moe_dispatch",
    )(dest, hn, base)


def _expert_body(te_ref, used_ref, x_ref, wg_ref, wu_ref, wd_ref, y_ref):
    i = pl.program_id(0)

    @pl.when(i < used_ref[0])
    def _():
        xb = x_ref[...].astype(BF16)
        g = _dot(xb, wg_ref[0])
        u = _dot(xb, wu_ref[0])
        h = (g / (1.0 + jnp.exp(-g)) * u).astype(BF16)
        y_ref[...] = _dot(h, wd_ref[0])

    @pl.when(i >= used_ref[0])
    def _():
        y_ref[...] = jnp.zeros_like(y_ref)


def _experts(xs, tile_expert, n_used, layer, w_gate, w_up, w_down):
    n_rows = xs.shape[0]
    tm = EXPERT_TILE
    expert = lambda i, te, nu: (layer, te[i], 0, 0)
    grid_spec = pltpu.PrefetchScalarGridSpec(
        num_scalar_prefetch=2,
        grid=(n_rows // tm,),
        in_specs=[pl.BlockSpec((tm, D_MODEL), lambda i, te, nu: (i, 0)),
                  pl.BlockSpec((None, 1, D_MODEL, D_EXPERT), expert),
                  pl.BlockSpec((None, 1, D_MODEL, D_EXPERT), expert),
                  pl.BlockSpec((None, 1, D_EXPERT, D_MODEL), expert)],
        out_specs=pl.BlockSpec((tm, D_MODEL), lambda i, te, nu: (i, 0)),
    )
    return pl.pallas_call(
        _expert_body,
        grid_spec=grid_spec,
        out_shape=jax.ShapeDtypeStruct((n_rows, D_MODEL), F32),
        compiler_params=_params("arbitrary"),
        name="moe_experts",
    )(tile_expert, n_used, xs, w_gate, w_up, w_down)


def _combine_body(dest_ref, x_ref, meta_ref, ys_ref, *rest, final):
    if final:
        g_ref, o_ref, ya_ref, yb_ref, sem = rest
    else:
        o_ref, ya_ref, yb_ref, sem = rest
    tm = x_ref.shape[0]

    def issue(k0, carry):
        for u in range(MOVE_UNROLL):
            k = k0 * MOVE_UNROLL + u
            _row_copy(ys_ref, dest_ref[0, 0, k], ya_ref, k, sem).start(priority=0)
            _row_copy(ys_ref, dest_ref[0, 0, tm + k], yb_ref, k, sem).start(priority=1)
        return carry

    lax.fori_loop(0, tm // MOVE_UNROLL, issue, 0)
    for buf in (ya_ref, yb_ref):
        pltpu.make_async_copy(ys_ref.at[pl.ds(0, tm), :], buf, sem).wait()
    meta = meta_ref[...]
    x = x_ref[...] + meta[:, 2:3] * ya_ref[...] + meta[:, 3:4] * yb_ref[...]
    o_ref[...] = _rms(x, g_ref[...]) if final else x


def _combine(x1, meta, dest, ys, r0, rows, final_gain=None):
    tm = MOVE_TILE
    first = r0 // tm
    final = final_gain is not None
    tile = lambda i: (first + i, 0)
    in_specs = [pl.BlockSpec((1, 1, 2 * tm), lambda i: (first + i, 0, 0), memory_space=pltpu.SMEM),
                pl.BlockSpec((tm, D_MODEL), tile),
                pl.BlockSpec((tm, LANES), tile),
                pl.BlockSpec(memory_space=pl.ANY)]
    args = [dest, x1, meta, ys]
    if final:
        in_specs.append(pl.BlockSpec((1, D_MODEL), lambda i: (0, 0)))
        args.append(final_gain.reshape(1, D_MODEL))
    return pl.pallas_call(
        functools.partial(_combine_body, final=final),
        grid=(rows // tm,),
        in_specs=in_specs,
        out_specs=pl.BlockSpec((tm, D_MODEL), lambda i: (i, 0)),
        out_shape=jax.ShapeDtypeStruct((rows, D_MODEL), F32),
        scratch_shapes=[pltpu.VMEM((tm, D_MODEL), F32), pltpu.VMEM((tm, D_MODEL), F32),
                        pltpu.SemaphoreType.DMA],
        compiler_params=_params("arbitrary"),
        name="moe_combine_final" if final else "moe_combine",
    )(*args)


def _moe(x1, hn, logits, layer, w_gate, w_up, w_down, spare, final=None):
    t = x1.shape[0]
    tm = MOVE_TILE
    meta, idx, counts = _route(logits)
    counts = counts[0, :N_EXPERTS].astype(I32)
    padded = (counts + EXPERT_TILE - 1) // EXPERT_TILE * EXPERT_TILE
    ends = jnp.cumsum(padded)
    offsets = ends - padded
    n_rows = 2 * t + N_EXPERTS * EXPERT_TILE
    n_tiles = n_rows // EXPERT_TILE
    n_used = (ends[-1] // EXPERT_TILE).astype(I32)
    tile_start = jnp.minimum(jnp.arange(n_tiles, dtype=I32), n_used - 1) * EXPERT_TILE
    tile_expert = jnp.sum((ends[None, :] <= tile_start[:, None]).astype(I32), axis=1)
    experts = jnp.arange(N_EXPERTS, dtype=I32)[None, :]

    def position(e_row, rank_row):
        e = idx[:, e_row, :].reshape(t).astype(I32)[:, None]
        rank = idx[:, rank_row, :].reshape(t).astype(I32)
        return jnp.sum(jnp.where(e == experts, offsets[None, :], 0), axis=1) + rank

    dest1, dest2 = position(0, 4), position(1, 5)

    def tiled(rows):
        return jnp.concatenate([dest1.reshape(t // rows, 1, rows), dest2.reshape(t // rows, 1, rows)], axis=-1)

    dest = tiled(tm)
    base = jnp.zeros((n_rows, D_MODEL), F32) if spare is None else spare
    xs = _dispatch(hn, tiled(DISPATCH_TILE), base)
    ys = _experts(xs, tile_expert, n_used.reshape(1), layer, w_gate, w_up, w_down)
    if final is None:
        return _combine(x1, meta, dest, ys, 0, t), ys
    gain, ranges = final
    return [_combine(x1, meta, dest, ys, r0, rows, gain) for r0, rows in ranges], ys


def kernel(x_prompt, x_sample, norm_mix, norm_ffn, norm_final, w_in_even, q_gain, k_gain, rpb, w_out_even,
           w_in_odd, lam_q1, lam_k1, lam_q2, lam_k2, subln_gain, w_out_odd,
           w_rg, b_rg, w_re, b_re, w_gate, w_up, w_down):
    bp, lp, d = x_prompt.shape
    bs, ls, _ = x_sample.shape
    assert d == D_MODEL and (bp * lp) % ls == 0 and lp % MOVE_TILE == 0 and ls % MOVE_TILE == 0
    assert (bp * lp + bs * ls) % DISPATCH_TILE == 0
    tp, ts = bp * lp, bs * ls
    t = tp + ts
    seqs = ((0, bp, lp), (tp, bs, ls))
    x = [x_prompt.reshape(tp, d), x_sample.reshape(ts, d)]
    max_len = max(lp, ls)
    depth = norm_mix.shape[0]

    def router(i):
        w = jnp.zeros((D_MODEL, LANES), F32)
        w = w.at[:, :N_GROUPS].set(w_rg[i]).at[:, N_GROUPS:N_GROUPS + N_EXPERTS].set(w_re[i])
        b = jnp.zeros((1, LANES), F32)
        b = b.at[0, :N_GROUPS].set(b_rg[i]).at[0, N_GROUPS:N_GROUPS + N_EXPERTS].set(b_re[i])
        w_hi = w.astype(BF16)
        w_lo = (w - w_hi.astype(F32)).astype(BF16)
        return jnp.concatenate([w_hi, w_lo], axis=1), b

    expert_w = (w_gate.astype(BF16), w_up.astype(BF16), w_down.astype(BF16))
    spare = None
    for i in range(depth):
        j = i // 2
        if i % 2 == 0:
            head_gain = jnp.stack([jnp.tile(q_gain[j], 2), jnp.tile(k_gain[j], 2)])
            a, b, avt = _inproj(x, norm_mix[i], w_in_even[j].astype(BF16), _axial_tables(max_len), seqs, head_gain)
            bias = _natten_bias(rpb[j])
            pieces = [[_gqa(a, avt, seq), _natten(b, bias, seq)] for seq in seqs]
            w_out = w_out_even[j]
        else:
            c, cvt = _inproj(x, norm_mix[i], w_in_odd[j].astype(BF16), _partial_tables(max_len), seqs)
            lam_init = 0.8 - 0.6 * math.exp(-0.3 * i)
            lam_rows = jnp.zeros((8, LANES), F32).at[:4, :HEAD_DIM].set(
                jnp.stack([lam_q1[j], lam_k1[j], lam_q2[j], lam_k2[j]]))
            pieces = [[_diff(c, cvt, lam_rows, subln_gain[j].reshape(1, LANES), lam_init, seq)] for seq in seqs]
            w_out = w_out_odd[j]
        w_router, b_router = router(i)
        x1, hn, logits = _outproj(pieces, w_out.astype(BF16), x, norm_ffn[i], w_router, b_router)
        final = (norm_final, ((0, tp), (tp, ts))) if i == depth - 1 else None
        x, spare = _moe(x1, hn, logits, i, *expert_w, spare, final)
        x = x if final else [x]
    y_prompt, y_sample = x
    return (y_prompt.reshape(bp, lp, d), y_sample.reshape(bs, ls, d))
```
